```python
import jax, jax.numpy as jnp
from jax import lax
import numpy as np

D_MODEL = 1024
BATCH = 4
SEQ = 4096
DEPTH = 1

CHUNK = 64
M_HEADS = 4
M_HEAD_DIM = D_MODEL // M_HEADS
M_WIDTH = M_HEADS * M_HEAD_DIM
CONV_K = 4
SGU_BLOCK = 128
SGU_GROUPS = 4
SGU_WIDTH = D_MODEL
SGU_GROUP_DIM = SGU_WIDTH // SGU_GROUPS
D_FF = 2816
PROJ_SIZES = (M_WIDTH, M_WIDTH, M_WIDTH, M_WIDTH, M_HEADS, M_HEADS, 2 * SGU_WIDTH, D_MODEL, D_MODEL)
N_PROJ = 4 * M_WIDTH + 2 * M_HEADS + 2 * SGU_WIDTH + 2 * D_MODEL
RMS_EPS = 1e-6
LN_EPS = 1e-5

kernel_name = "hybrid_mlstm_sgu_macaron_block"


def rmsnorm(x, g):
    x32 = x.astype(jnp.float32)
    y = x32 * lax.rsqrt(jnp.mean(x32 * x32, axis=-1, keepdims=True) + RMS_EPS)
    return (y * g.astype(jnp.float32)).astype(x.dtype)


def layernorm32(x, eps):
    x32 = x.astype(jnp.float32)
    xc = x32 - jnp.mean(x32, axis=-1, keepdims=True)
    return xc * lax.rsqrt(jnp.mean(xc * xc, axis=-1, keepdims=True) + eps)


def swiglu(x, w_in, w_out):
    gate, up = jnp.split(x @ w_in, 2, axis=-1)
    return (jax.nn.silu(gate) * up) @ w_out


def causal_depthwise_conv(x, w, b):
    y = lax.conv_general_dilated(
        x, w[:, None, :].astype(x.dtype), window_strides=(1,),
        padding=[(CONV_K - 1, 0)], dimension_numbers=('NWC', 'WIO', 'NWC'),
        feature_group_count=x.shape[-1])
    return y + b.astype(x.dtype)


def mlstm_chunkwise(q, k, v, log_i, log_f):
    bsz, nh, seqlen, dh = q.shape
    nc = seqlen // CHUNK

    def to_chunks(t):
        return jnp.moveaxis(t.reshape(bsz, nh, nc, CHUNK, *t.shape[3:]), 2, 0)

    causal = jnp.tril(jnp.ones((CHUNK, CHUNK), dtype=bool))

    def step(carry, xs):
        c_state, n_state, m_state = carry
        qc, kc, vc, lic, lfc = xs
        b = jnp.cumsum(lfc, axis=-1)
        a_inter = b + m_state[..., None]
        d = jnp.where(causal, b[..., :, None] - b[..., None, :] + lic[..., None, :], -jnp.inf)
        m_t = jnp.maximum(a_inter, jnp.max(d, axis=-1))
        w_inter = jnp.exp(a_inter - m_t)
        scores = jnp.einsum('bhtd,bhsd->bhts', qc, kc) * jnp.exp(d - m_t[..., None])
        num = (w_inter[..., None] * jnp.einsum('bhed,bhtd->bhte', c_state, qc)
               + jnp.einsum('bhts,bhse->bhte', scores, vc))
        den = w_inter * jnp.einsum('bhd,bhtd->bht', n_state, qc) + jnp.sum(scores, axis=-1)
        h = num / jnp.maximum(jnp.abs(den), jnp.exp(-m_t))[..., None]
        b_last = b[..., -1]
        g = b_last[..., None] - b + lic
        m_new = jnp.maximum(b_last + m_state, jnp.max(g, axis=-1))
        decay = jnp.exp(b_last + m_state - m_new)
        wk = jnp.exp(g - m_new[..., None])
        c_new = decay[..., None, None] * c_state + jnp.einsum('bhs,bhse,bhsd->bhed', wk, vc, kc)
        n_new = decay[..., None] * n_state + jnp.einsum('bhs,bhsd->bhd', wk, kc)
        return (c_new, n_new, m_new), h

    init = (jnp.zeros((bsz, nh, dh, dh), jnp.float32),
            jnp.zeros((bsz, nh, dh), jnp.float32),
            jnp.zeros((bsz, nh), jnp.float32))
    xs = (to_chunks(q), to_chunks(k), to_chunks(v), to_chunks(log_i), to_chunks(log_f))
    _, hs = lax.scan(step, init, xs)
    return jnp.moveaxis(hs, 0, 2).reshape(bsz, nh, seqlen, dh)


def mlstm_branch(q_pre, k_pre, v_pre, o_pre, i_pre, f_pre, conv_w, conv_b, b_i, b_f, head_g):
    bsz, seqlen, _ = q_pre.shape
    qk = jax.nn.silu(causal_depthwise_conv(jnp.concatenate([q_pre, k_pre], axis=-1), conv_w, conv_b))
    q, k = jnp.split(qk, 2, axis=-1)

    def heads(t):
        return t.reshape(bsz, seqlen, M_HEADS, M_HEAD_DIM).transpose(0, 2, 1, 3).astype(jnp.float32)

    log_i = (i_pre + b_i).astype(jnp.float32).transpose(0, 2, 1)
    log_f = jax.nn.log_sigmoid((f_pre + b_f).astype(jnp.float32)).transpose(0, 2, 1)
    h = mlstm_chunkwise(heads(q), heads(k) * (M_HEAD_DIM ** -0.5), heads(v_pre), log_i, log_f)
    h = layernorm32(h, LN_EPS).transpose(0, 2, 1, 3).reshape(bsz, seqlen, M_WIDTH)
    return (jax.nn.sigmoid(o_pre.astype(jnp.float32)) * h * head_g).astype(q_pre.dtype)


def sgu_branch(z_pre, norm_g, norm_b, w_s, b_s):
    bsz, seqlen, _ = z_pre.shape
    u, v = jnp.split(jax.nn.gelu(z_pre, approximate=False), 2, axis=-1)
    v = (layernorm32(v, LN_EPS) * norm_g + norm_b).astype(z_pre.dtype)
    nb = seqlen // SGU_BLOCK
    v = v.reshape(bsz, nb, SGU_BLOCK, SGU_GROUPS, SGU_GROUP_DIM)
    chunk_id = jnp.arange(SGU_BLOCK) // CHUNK
    mask = chunk_id[:, None] >= chunk_id[None, :]
    w = jnp.where(mask, w_s, 0.0).astype(v.dtype)
    sv = jnp.einsum('gts,bnsgc->bntgc', w, v) + b_s.T[:, :, None].astype(v.dtype)
    return u * sv.reshape(bsz, seqlen, SGU_WIDTH)


def setup_inputs(seed: int = 0) -> dict:
    key = jax.random.key(seed)
    ks = jax.random.split(key, 20)
    L = DEPTH

    def nrm(k, shape, scale):
        return jax.random.normal(k, shape, jnp.float32) * scale

    return {
        'x': nrm(ks[0], (BATCH, SEQ, D_MODEL), 1.0),
        'ffn1_norm_g': 1.0 + nrm(ks[1], (L, D_MODEL), 0.05),
        'ffn1_w_in': nrm(ks[2], (L, D_MODEL, 2 * D_FF), D_MODEL ** -0.5),
        'ffn1_w_out': nrm(ks[3], (L, D_FF, D_MODEL), D_FF ** -0.5),
        'mix_norm_g': 1.0 + nrm(ks[4], (L, D_MODEL), 0.05),
        'w_mix_in': nrm(ks[5], (L, D_MODEL, N_PROJ), D_MODEL ** -0.5),
        'mlstm_conv_w': nrm(ks[6], (L, CONV_K, 2 * M_WIDTH), CONV_K ** -0.5),
        'mlstm_conv_b': nrm(ks[7], (L, 2 * M_WIDTH), 0.01),
        'mlstm_b_igate': nrm(ks[8], (L, M_HEADS), 0.1),
        'mlstm_b_fgate': jnp.linspace(3.0, 6.0, M_HEADS, dtype=jnp.float32) + nrm(ks[9], (L, M_HEADS), 0.1),
        'mlstm_head_norm_g': 1.0 + nrm(ks[10], (L, M_WIDTH), 0.05),
        'sgu_norm_g': 1.0 + nrm(ks[11], (L, SGU_WIDTH), 0.05),
        'sgu_norm_b': nrm(ks[12], (L, SGU_WIDTH), 0.01),
        'sgu_w_s': nrm(ks[13], (L, SGU_GROUPS, SGU_BLOCK, SGU_BLOCK), SGU_BLOCK ** -0.5),
        'sgu_b_s': 1.0 + nrm(ks[14], (L, SGU_GROUPS, SGU_BLOCK), 0.1),
        'w_mix_out': nrm(ks[15], (L, D_MODEL, D_MODEL), D_MODEL ** -0.5),
        'ffn2_norm_g': 1.0 + nrm(ks[16], (L, D_MODEL), 0.05),
        'ffn2_w_in': nrm(ks[17], (L, D_MODEL, 2 * D_FF), D_MODEL ** -0.5),
        'ffn2_w_out': nrm(ks[18], (L, D_FF, D_MODEL), D_FF ** -0.5),
        'final_norm_g': 1.0 + nrm(ks[19], (D_MODEL,), 0.05),
    }


def reference(x, ffn1_norm_g, ffn1_w_in, ffn1_w_out, mix_norm_g, w_mix_in, mlstm_conv_w, mlstm_conv_b,
              mlstm_b_igate, mlstm_b_fgate, mlstm_head_norm_g, sgu_norm_g, sgu_norm_b, sgu_w_s, sgu_b_s,
              w_mix_out, ffn2_norm_g, ffn2_w_in, ffn2_w_out, final_norm_g):
    split_idx = np.cumsum(PROJ_SIZES)[:-1].tolist()
    for l in range(DEPTH):
        x = x + 0.5 * swiglu(rmsnorm(x, ffn1_norm_g[l]), ffn1_w_in[l], ffn1_w_out[l])
        xn = rmsnorm(x, mix_norm_g[l])
        proj = xn @ w_mix_in[l]
        q_pre, k_pre, v_pre, o_pre, i_pre, f_pre, z_pre, gate_m, gate_s = jnp.split(proj, split_idx, axis=-1)
        h_m = mlstm_branch(q_pre, k_pre, v_pre, o_pre, i_pre, f_pre, mlstm_conv_w[l], mlstm_conv_b[l],
                           mlstm_b_igate[l], mlstm_b_fgate[l], mlstm_head_norm_g[l])
        h_s = sgu_branch(z_pre, sgu_norm_g[l], sgu_norm_b[l], sgu_w_s[l], sgu_b_s[l])
        merged = jax.nn.sigmoid(gate_m) * h_m + jax.nn.sigmoid(gate_s) * h_s
        x = x + merged @ w_mix_out[l]
        x = x + 0.5 * swiglu(rmsnorm(x, ffn2_norm_g[l]), ffn2_w_in[l], ffn2_w_out[l])
    return rmsnorm(x, final_norm_g)
```

```python
import functools

import jax
import jax.numpy as jnp
from jax import lax
from jax.experimental import pallas as pl
from jax.experimental.pallas import tpu as pltpu

D_MODEL = 1024
CHUNK = 64
M_HEADS = 4
M_HEAD_DIM = D_MODEL // M_HEADS
M_WIDTH = M_HEADS * M_HEAD_DIM
CONV_K = 4
SGU_BLOCK = 128
SGU_GROUPS = 4
SGU_WIDTH = D_MODEL
SGU_GROUP_DIM = SGU_WIDTH // SGU_GROUPS
D_FF = 2816
RMS_EPS = 1e-6
LN_EPS = 1e-5

MC = 256
GATE_PAD = 128
SUBLANES = 8
FF_TILE = 256
VMEM_LIMIT = 56 * 1024 * 1024

F32 = jnp.float32
BF16 = jnp.bfloat16


def _dot(a, b):
    return jnp.dot(a, b, preferred_element_type=F32)


def _rmsnorm(x, g):
    return x * lax.rsqrt(jnp.mean(x * x, axis=-1, keepdims=True) + RMS_EPS) * g


def _layernorm(x, eps):
    xc = x - jnp.mean(x, axis=-1, keepdims=True)
    return xc * lax.rsqrt(jnp.mean(xc * xc, axis=-1, keepdims=True) + eps)


def _sigmoid(x):
    return 1.0 / (1.0 + jnp.exp(-x))


def _ffn_kernel(x_ref, g_ref, win_ref, wout_ref, fg_ref, o_ref, h_ref, *, final_norm):
    x = x_ref[...]
    xb = _rmsnorm(x, g_ref[...]).astype(BF16)
    for j in range(D_FF // FF_TILE):
        lo = j * FF_TILE
        gate = _dot(xb, win_ref[:, lo:lo + FF_TILE])
        up = _dot(xb, win_ref[:, D_FF + lo:D_FF + lo + FF_TILE])
        h_ref[:, lo:lo + FF_TILE] = (gate * _sigmoid(gate) * up).astype(BF16)
    y = x + 0.5 * _dot(h_ref[...], wout_ref[...])
    if final_norm:
        y = _rmsnorm(y, fg_ref[...])
    o_ref[...] = y


def _ffn(x2d, g, w_in, w_out, final_g, *, final_norm, tm=512):
    t = x2d.shape[0]
    const = lambda i: (0, 0)
    return pl.pallas_call(
        functools.partial(_ffn_kernel, final_norm=final_norm),
        grid=(t // tm,),
        in_specs=[
            pl.BlockSpec((tm, D_MODEL), lambda i: (i, 0)),
            pl.BlockSpec((1, D_MODEL), const),
            pl.BlockSpec((D_MODEL, 2 * D_FF), const, pipeline_mode=pl.Buffered(1)),
            pl.BlockSpec((D_FF, D_MODEL), const, pipeline_mode=pl.Buffered(1)),
            pl.BlockSpec((1, D_MODEL), const),
        ],
        out_specs=pl.BlockSpec((tm, D_MODEL), lambda i: (i, 0)),
        out_shape=jax.ShapeDtypeStruct((t, D_MODEL), F32),
        scratch_shapes=[pltpu.VMEM((tm, D_FF), BF16)],
        compiler_params=pltpu.CompilerParams(
            dimension_semantics=("parallel",), vmem_limit_bytes=VMEM_LIMIT),
        name="ffn_final" if final_norm else "ffn",
    )(x2d, g, w_in, w_out, final_g)


_QK0, _V0, _O0, _Z0, _GM0, _GS0, _IF0 = 0, 2048, 3072, 4096, 6144, 7168, 8192
N_PACK = 8192 + GATE_PAD


def _proj_kernel(x_ref, g_ref, w_ref, cw_ref, cb_ref, hg_ref, sg_ref, sb_ref, ws_ref, bst_ref,
                 q_ref, k_ref, v_ref, gif_ref, gmo_ref, hs_ref, buf_ref, *, tm):
    s = pl.program_id(1)
    xb = _rmsnorm(x_ref[...], g_ref[...]).astype(BF16)

    @pl.when(s == 0)
    def _():
        buf_ref[0:SUBLANES, :] = jnp.zeros((SUBLANES, 2 * M_WIDTH), F32)

    qk = _dot(xb, w_ref[:, _QK0:_QK0 + 2 * M_WIDTH])
    buf_ref[SUBLANES:SUBLANES + tm, :] = qk
    cw = cw_ref[...]
    acc = qk * cw[CONV_K - 1:CONV_K, :] + cb_ref[...]
    for j in range(CONV_K - 1):
        off = SUBLANES - (CONV_K - 1) + j
        acc = acc + buf_ref[off:off + tm, :] * cw[j:j + 1, :]
    buf_ref[0:SUBLANES, :] = buf_ref[tm:tm + SUBLANES, :]
    qkc = acc * _sigmoid(acc)
    q_ref[...] = qkc[:, :M_WIDTH].astype(BF16)
    k_ref[...] = (qkc[:, M_WIDTH:] * (M_HEAD_DIM ** -0.5)).astype(BF16)

    v_ref[...] = _dot(xb, w_ref[:, _V0:_V0 + M_WIDTH]).astype(BF16)
    o_pre = _dot(xb, w_ref[:, _O0:_O0 + M_WIDTH])
    gate_m = _dot(xb, w_ref[:, _GM0:_GM0 + D_MODEL])
    gmo_ref[...] = _sigmoid(gate_m) * _sigmoid(o_pre) * hg_ref[...]
    gif_ref[...] = _dot(xb, w_ref[:, _IF0:_IF0 + GATE_PAD])

    z = _dot(xb, w_ref[:, _Z0:_Z0 + 2 * SGU_WIDTH])
    zg = 0.5 * z * (1.0 + lax.erf(z * (2.0 ** -0.5)))
    u = zg[:, :SGU_WIDTH]
    vn = (_layernorm(zg[:, SGU_WIDTH:], LN_EPS) * sg_ref[...] + sb_ref[...]).astype(BF16)
    gs = _sigmoid(_dot(xb, w_ref[:, _GS0:_GS0 + D_MODEL]))
    gu = gs * u
    ti = lax.broadcasted_iota(jnp.int32, (SGU_BLOCK, SGU_BLOCK), 0) // CHUNK
    si = lax.broadcasted_iota(jnp.int32, (SGU_BLOCK, SGU_BLOCK), 1) // CHUNK
    mask = ti >= si
    for g in range(SGU_GROUPS):
        wg = jnp.where(mask, ws_ref[g], 0.0).astype(BF16)
        bg = bst_ref[:, g:g + 1]
        cs = slice(g * SGU_GROUP_DIM, (g + 1) * SGU_GROUP_DIM)
        for nb in range(tm // SGU_BLOCK):
            rs = slice(nb * SGU_BLOCK, (nb + 1) * SGU_BLOCK)
            sv = _dot(wg, vn[rs, cs]) + bg
            hs_ref[rs, cs] = gu[rs, cs] * sv


def _proj(x1, g, w_pack, conv_w, conv_b, head_g, sgu_g, sgu_b, w_s, b_s_t, *, bsz, seq, tm=256):
    t = bsz * seq
    ns = seq // tm
    row = lambda b, s: (b * ns + s, 0)
    c2 = lambda b, s: (0, 0)
    c3 = lambda b, s: (0, 0, 0)
    wide = lambda n, dt: jax.ShapeDtypeStruct((t, n), dt)
    return pl.pallas_call(
        functools.partial(_proj_kernel, tm=tm),
        grid=(bsz, ns),
        in_specs=[
            pl.BlockSpec((tm, D_MODEL), row),
            pl.BlockSpec((1, D_MODEL), c2),
            pl.BlockSpec((D_MODEL, N_PACK), c2, pipeline_mode=pl.Buffered(1)),
            pl.BlockSpec((CONV_K, 2 * M_WIDTH), c2),
            pl.BlockSpec((1, 2 * M_WIDTH), c2),
            pl.BlockSpec((1, M_WIDTH), c2),
            pl.BlockSpec((1, SGU_WIDTH), c2),
            pl.BlockSpec((1, SGU_WIDTH), c2),
            pl.BlockSpec((SGU_GROUPS, SGU_BLOCK, SGU_BLOCK), c3),
            pl.BlockSpec((SGU_BLOCK, SGU_GROUPS), c2),
        ],
        out_specs=[
            pl.BlockSpec((tm, M_WIDTH), row),
            pl.BlockSpec((tm, M_WIDTH), row),
            pl.BlockSpec((tm, M_WIDTH), row),
            pl.BlockSpec((tm, GATE_PAD), row),
            pl.BlockSpec((tm, D_MODEL), row),
            pl.BlockSpec((tm, D_MODEL), row),
        ],
        out_shape=[wide(M_WIDTH, BF16), wide(M_WIDTH, BF16), wide(M_WIDTH, BF16),
                   wide(GATE_PAD, F32), wide(D_MODEL, F32), wide(D_MODEL, F32)],
        scratch_shapes=[pltpu.VMEM((tm + 2 * SUBLANES, 2 * M_WIDTH), F32)],
        compiler_params=pltpu.CompilerParams(
            dimension_semantics=("arbitrary", "arbitrary"), vmem_limit_bytes=VMEM_LIMIT),
        name="proj",
    )(x1, g, w_pack, conv_w, conv_b, head_g, sgu_g, sgu_b, w_s, b_s_t)


def _prefix_scan(x, op, lane):
    n = x.shape[-1]
    sh = 1
    while sh < n:
        x = jnp.where(lane >= sh, op(x, pltpu.roll(x, sh, axis=1)), x)
        sh *= 2
    return x


def _gates_kernel(ip_ref, fp_ref, bi_ref, bf_ref,
                  negmx_ref, wint_ref, en_ref, wk_ref, r_ref, decay_ref, *, nc, nbh):
    lane = lax.broadcasted_iota(jnp.int32, ip_ref.shape, 1)
    li = ip_ref[...] + bi_ref[...]
    fx = fp_ref[...] + bf_ref[...]
    lf = jnp.minimum(fx, 0.0) - jnp.log1p(jnp.exp(-jnp.abs(fx)))
    b = _prefix_scan(lf, jnp.add, lane)
    r = li - b
    cm = _prefix_scan(r, jnp.maximum, lane)
    b_last = b[:, MC - 1:MC]
    cm_last = cm[:, MC - 1:MC]
    m = jnp.zeros((nbh, 1), F32)
    ms = []
    for c in range(nc):
        ms.append(m)
        rs = slice(c * nbh, (c + 1) * nbh)
        m = b_last[rs] + jnp.maximum(m, cm_last[rs])
    m_in = jnp.concatenate(ms, axis=0)
    mx = jnp.maximum(m_in, cm)
    mx_last = mx[:, MC - 1:MC]
    negmx_ref[...] = -mx
    wint_ref[...] = jnp.exp(m_in - mx)
    en_ref[...] = jnp.exp(-(b + mx))
    wk_ref[...] = jnp.exp(r - mx_last)
    r_ref[...] = r
    decay_ref[...] = jnp.broadcast_to(jnp.exp(m_in - mx_last), decay_ref.shape)


def _gates(ip, fp, bi, bf, *, nc, nbh):
    n = ip.shape[0]
    full = jax.ShapeDtypeStruct((n, MC), F32)
    return pl.pallas_call(
        functools.partial(_gates_kernel, nc=nc, nbh=nbh),
        out_shape=[full] * 6,
        compiler_params=pltpu.CompilerParams(vmem_limit_bytes=VMEM_LIMIT),
        name="gates",
    )(ip, fp, bi, bf)


def _mlstm_kernel(q_ref, k_ref, v_ref, cols_ref, rows_ref, gmo_ref, hs_ref, x_ref, wo_ref,
                  o_ref, ct_ref, n_ref, hm_ref):
    @pl.when(pl.program_id(1) == 0)
    def _():
        ct_ref[...] = jnp.zeros(ct_ref.shape, F32)
        n_ref[...] = jnp.zeros(n_ref.shape, F32)

    ti = lax.broadcasted_iota(jnp.int32, (MC, MC), 0)
    si = lax.broadcasted_iota(jnp.int32, (MC, MC), 1)
    causal = si <= ti
    for h in range(M_HEADS):
        sl = slice(h * M_HEAD_DIM, (h + 1) * M_HEAD_DIM)
        q = q_ref[:, sl]
        k = k_ref[:, sl]
        v = v_ref[:, sl]
        negmx = cols_ref[:, h:h + 1]
        w_int = cols_ref[:, M_HEADS + h:M_HEADS + h + 1]
        en = cols_ref[:, 2 * M_HEADS + h:2 * M_HEADS + h + 1]
        wk = cols_ref[:, 3 * M_HEADS + h:3 * M_HEADS + h + 1]
        r = rows_ref[h:h + 1, :]
        decay = rows_ref[M_HEADS + h:M_HEADS + h + 1, :]

        s = lax.dot_general(q, k, (((1,), (1,)), ((), ())), preferred_element_type=F32)
        p = s * jnp.exp(jnp.where(causal, negmx + r, -jnp.inf))
        ct = ct_ref[h]
        n = n_ref[h:h + 1, :]
        num = w_int * _dot(q, ct.astype(BF16)) + _dot(p.astype(BF16), v)
        qn = jnp.sum(q.astype(F32) * n, axis=-1, keepdims=True)
        den = w_int * qn + jnp.sum(p, axis=-1, keepdims=True)
        hh = num / jnp.maximum(jnp.abs(den), en)
        hm_ref[:, sl] = _layernorm(hh, LN_EPS)

        kw = k.astype(F32) * wk
        ct_ref[h] = decay * ct + _dot(kw.T.astype(BF16), v)
        n_ref[h:h + 1, :] = decay * n + jnp.sum(kw, axis=0, keepdims=True)

    merged = gmo_ref[...] * hm_ref[...] + hs_ref[...]
    o_ref[...] = x_ref[...] + _dot(merged.astype(BF16), wo_ref[...])


def _mlstm(q, k, v, cols, rows, gmo, hs, x1, w_out, *, bsz, seq):
    t = bsz * seq
    nc = seq // MC
    row = lambda b, c: (b * nc + c, 0)
    return pl.pallas_call(
        _mlstm_kernel,
        grid=(bsz, nc),
        in_specs=[
            pl.BlockSpec((MC, M_WIDTH), row),
            pl.BlockSpec((MC, M_WIDTH), row),
            pl.BlockSpec((MC, M_WIDTH), row),
            pl.BlockSpec((MC, 4 * M_HEADS), row),
            pl.BlockSpec((None, None, 2 * M_HEADS, MC), lambda b, c: (b, c, 0, 0)),
            pl.BlockSpec((MC, D_MODEL), row),
            pl.BlockSpec((MC, D_MODEL), row),
            pl.BlockSpec((MC, D_MODEL), row),
            pl.BlockSpec((D_MODEL, D_MODEL), lambda b, c: (0, 0)),
        ],
        out_specs=pl.BlockSpec((MC, D_MODEL), row),
        out_shape=jax.ShapeDtypeStruct((t, D_MODEL), F32),
        scratch_shapes=[
            pltpu.VMEM((M_HEADS, M_HEAD_DIM, M_HEAD_DIM), F32),
            pltpu.VMEM((SUBLANES, M_HEAD_DIM), F32),
            pltpu.VMEM((MC, M_WIDTH), F32),
        ],
        compiler_params=pltpu.CompilerParams(
            dimension_semantics=("arbitrary", "arbitrary"), vmem_limit_bytes=VMEM_LIMIT),
        name="mlstm",
    )(q, k, v, cols, rows, gmo, hs, x1, w_out)


def _pack_proj_weight(w):
    o_end = 4 * M_WIDTH
    z0 = o_end + 2 * M_HEADS
    gate_cols = jnp.pad(w[:, o_end:z0], ((0, 0), (0, GATE_PAD - 2 * M_HEADS)))
    return jnp.concatenate([w[:, :o_end], w[:, z0:], gate_cols], axis=1).astype(BF16)


def kernel(x, ffn1_norm_g, ffn1_w_in, ffn1_w_out, mix_norm_g, w_mix_in, mlstm_conv_w, mlstm_conv_b, mlstm_b_igate, mlstm_b_fgate, mlstm_head_norm_g, sgu_norm_g, sgu_norm_b, sgu_w_s, sgu_b_s, w_mix_out, ffn2_norm_g, ffn2_w_in, ffn2_w_out, final_norm_g):
    bsz, seq, _ = x.shape
    nc = seq // MC
    nbh = bsz * M_HEADS
    h = x.reshape(bsz * seq, D_MODEL)
    fg = final_norm_g.reshape(1, D_MODEL)
    for l in range(ffn1_w_in.shape[0]):
        h = _ffn(h, ffn1_norm_g[l].reshape(1, -1), ffn1_w_in[l].astype(BF16), ffn1_w_out[l].astype(BF16),
                 fg, final_norm=False)

        q, k, v, gif, gmo, hs = _proj(
            h, mix_norm_g[l].reshape(1, -1), _pack_proj_weight(w_mix_in[l]),
            mlstm_conv_w[l], mlstm_conv_b[l].reshape(1, -1), mlstm_head_norm_g[l].reshape(1, -1),
            sgu_norm_g[l].reshape(1, -1), sgu_norm_b[l].reshape(1, -1), sgu_w_s[l], sgu_b_s[l].T,
            bsz=bsz, seq=seq)

        gi = gif[:, :2 * M_HEADS].reshape(bsz, nc, MC, 2, M_HEADS).transpose(3, 1, 0, 4, 2)
        gi = gi.reshape(2, nc * nbh, MC)
        bi = jnp.tile(mlstm_b_igate[l], nc * bsz).reshape(-1, 1)
        bf = jnp.tile(mlstm_b_fgate[l], nc * bsz).reshape(-1, 1)
        negmx, wint, en, wk, r, decay = _gates(gi[0], gi[1], bi, bf, nc=nc, nbh=nbh)
        cols = jnp.stack([negmx, wint, en, wk]).reshape(4, nc, bsz, M_HEADS, MC)
        cols = cols.transpose(2, 1, 4, 0, 3).reshape(bsz * seq, 4 * M_HEADS)
        rows = jnp.stack([r, decay]).reshape(2, nc, bsz, M_HEADS, MC)
        rows = rows.transpose(2, 1, 0, 3, 4).reshape(bsz, nc, 2 * M_HEADS, MC)

        h = _mlstm(q, k, v, cols, rows, gmo, hs, h, w_mix_out[l].astype(BF16), bsz=bsz, seq=seq)

        last = l == ffn1_w_in.shape[0] - 1
        h = _ffn(h, ffn2_norm_g[l].reshape(1, -1), ffn2_w_in[l].astype(BF16), ffn2_w_out[l].astype(BF16),
                 fg, final_norm=last)
    return h.reshape(bsz, seq, D_MODEL)
```

```python
import functools

import jax
import jax.numpy as jnp
from jax import lax
from jax.experimental import pallas as pl
from jax.experimental.pallas import tpu as pltpu

D_MODEL = 1024
CHUNK = 64
M_HEADS = 4
M_HEAD_DIM = D_MODEL // M_HEADS
M_WIDTH = M_HEADS * M_HEAD_DIM
CONV_K = 4
SGU_BLOCK = 128
SGU_GROUPS = 4
SGU_WIDTH = D_MODEL
SGU_GROUP_DIM = SGU_WIDTH // SGU_GROUPS
D_FF = 2816
RMS_EPS = 1e-6
LN_EPS = 1e-5

MC = 256
GATE_PAD = 128
SUBLANES = 8
LANES = 128
FF_TILE = 256
VMEM_LIMIT = 56 * 1024 * 1024

F32 = jnp.float32
BF16 = jnp.bfloat16


def _dot(a, b):
    return jnp.dot(a, b, preferred_element_type=F32)


def _rmsnorm(x, g):
    return x * lax.rsqrt(jnp.mean(x * x, axis=-1, keepdims=True) + RMS_EPS) * g


def _layernorm(x, eps):
    xc = x - jnp.mean(x, axis=-1, keepdims=True)
    return xc * lax.rsqrt(jnp.mean(xc * xc, axis=-1, keepdims=True) + eps)


def _sigmoid(x):
    return 1.0 / (1.0 + jnp.exp(-x))


def _ffn_kernel(x_ref, g_ref, win_ref, wout_ref, fg_ref, o_ref, h_ref, *, final_norm):
    x = x_ref[...]
    xb = _rmsnorm(x, g_ref[...]).astype(BF16)
    for j in range(D_FF // FF_TILE):
        lo = j * FF_TILE
        gate = _dot(xb, win_ref[:, lo:lo + FF_TILE])
        up = _dot(xb, win_ref[:, D_FF + lo:D_FF + lo + FF_TILE])
        h_ref[:, lo:lo + FF_TILE] = (gate * _sigmoid(gate) * up).astype(BF16)
    y = x + 0.5 * _dot(h_ref[...], wout_ref[...])
    if final_norm:
        y = _rmsnorm(y, fg_ref[...])
    o_ref[...] = y


def _ffn(x2d, g, w_in, w_out, final_g, *, final_norm, tm=512):
    t = x2d.shape[0]
    const = lambda i: (0, 0)
    return pl.pallas_call(
        functools.partial(_ffn_kernel, final_norm=final_norm),
        grid=(t // tm,),
        in_specs=[
            pl.BlockSpec((tm, D_MODEL), lambda i: (i, 0)),
            pl.BlockSpec((1, D_MODEL), const),
            pl.BlockSpec((D_MODEL, 2 * D_FF), const, pipeline_mode=pl.Buffered(1)),
            pl.BlockSpec((D_FF, D_MODEL), const, pipeline_mode=pl.Buffered(1)),
            pl.BlockSpec((1, D_MODEL), const),
        ],
        out_specs=pl.BlockSpec((tm, D_MODEL), lambda i: (i, 0)),
        out_shape=jax.ShapeDtypeStruct((t, D_MODEL), F32),
        scratch_shapes=[pltpu.VMEM((tm, D_FF), BF16)],
        compiler_params=pltpu.CompilerParams(
            dimension_semantics=("parallel",), vmem_limit_bytes=VMEM_LIMIT),
        name="ffn_final" if final_norm else "ffn",
    )(x2d, g, w_in, w_out, final_g)


_QK0, _V0, _O0, _Z0, _GM0, _GS0, _IF0 = 0, 2048, 3072, 4096, 6144, 7168, 8192
N_PACK = 8192 + GATE_PAD


def _prefix_scan(x, op, lane):
    n = x.shape[-1]
    sh = 1
    while sh < n:
        x = jnp.where(lane >= sh, op(x, pltpu.roll(x, sh, axis=1)), x)
        sh *= 2
    return x


def _gate_vectors(gif, gb, m_ref):
    row = lax.broadcasted_iota(jnp.int32, (SUBLANES, MC), 0)
    lane = lax.broadcasted_iota(jnp.int32, (SUBLANES, MC), 1)
    head_rows = row < M_HEADS
    g8 = gif.T[0:SUBLANES, :] + gb
    lsig = jnp.minimum(g8, 0.0) - jnp.log1p(jnp.exp(-jnp.abs(g8)))
    lf = jnp.where(head_rows, pltpu.roll(lsig, M_HEADS, axis=0), 0.0)
    li = jnp.where(head_rows, g8, 0.0)
    b = _prefix_scan(lf, jnp.add, lane)
    r = li - b
    cm = _prefix_scan(r, jnp.maximum, lane)
    m_in = m_ref[:, 0:1]
    mx = jnp.maximum(m_in, cm)
    mx_last = mx[:, MC - 1:MC]
    m_ref[...] = jnp.broadcast_to(b[:, MC - 1:MC] + mx_last, m_ref.shape)
    cols = jnp.concatenate(
        [-mx, jnp.exp(m_in - mx), jnp.exp(-(b + mx)), jnp.exp(r - mx_last),
         jnp.zeros((LANES - 4 * SUBLANES, MC), F32)], axis=0).T
    return cols, r, jnp.exp(m_in - mx_last)


def _mixer_kernel(x_ref, g_ref, w_ref, cw_ref, cb_ref, gb_ref, hg_ref, sg_ref, sb_ref, ws_ref, bst_ref, wo_ref,
                  o_ref, buf_ref, ct_ref, n_ref, m_ref, mg_ref):
    @pl.when(pl.program_id(1) == 0)
    def _():
        buf_ref[0:SUBLANES, :] = jnp.zeros((SUBLANES, 2 * M_WIDTH), F32)
        ct_ref[...] = jnp.zeros(ct_ref.shape, F32)
        n_ref[...] = jnp.zeros(n_ref.shape, F32)
        m_ref[...] = jnp.zeros(m_ref.shape, F32)

    x = x_ref[...]
    xb = _rmsnorm(x, g_ref[...]).astype(BF16)

    cols, r_rows, decay_col = _gate_vectors(_dot(xb, w_ref[:, _IF0:_IF0 + GATE_PAD]), gb_ref[...], m_ref)

    qk = _dot(xb, w_ref[:, _QK0:_QK0 + 2 * M_WIDTH])
    buf_ref[SUBLANES:SUBLANES + MC, :] = qk
    cw = cw_ref[...]
    acc = qk * cw[CONV_K - 1:CONV_K, :] + cb_ref[...]
    for j in range(CONV_K - 1):
        off = SUBLANES - (CONV_K - 1) + j
        acc = acc + buf_ref[off:off + MC, :] * cw[j:j + 1, :]
    buf_ref[0:SUBLANES, :] = buf_ref[MC:MC + SUBLANES, :]
    qkc = acc * _sigmoid(acc)
    qa = qkc[:, :M_WIDTH].astype(BF16)
    ka = (qkc[:, M_WIDTH:] * (M_HEAD_DIM ** -0.5)).astype(BF16)
    va = _dot(xb, w_ref[:, _V0:_V0 + M_WIDTH]).astype(BF16)

    z = _dot(xb, w_ref[:, _Z0:_Z0 + 2 * SGU_WIDTH])
    zg = 0.5 * z * (1.0 + lax.erf(z * (2.0 ** -0.5)))
    vn = (_layernorm(zg[:, SGU_WIDTH:], LN_EPS) * sg_ref[...] + sb_ref[...]).astype(BF16)
    gu = _sigmoid(_dot(xb, w_ref[:, _GS0:_GS0 + D_MODEL])) * zg[:, :SGU_WIDTH]
    ti = lax.broadcasted_iota(jnp.int32, (SGU_BLOCK, SGU_BLOCK), 0) // CHUNK
    si = lax.broadcasted_iota(jnp.int32, (SGU_BLOCK, SGU_BLOCK), 1) // CHUNK
    blk_mask = ti >= si
    for g in range(SGU_GROUPS):
        wg = jnp.where(blk_mask, ws_ref[g], 0.0).astype(BF16)
        bg = bst_ref[:, g:g + 1]
        cs = slice(g * SGU_GROUP_DIM, (g + 1) * SGU_GROUP_DIM)
        for nb in range(MC // SGU_BLOCK):
            rs = slice(nb * SGU_BLOCK, (nb + 1) * SGU_BLOCK)
            mg_ref[rs, cs] = gu[rs, cs] * (_dot(wg, vn[rs, cs]) + bg)

    gmo = (_sigmoid(_dot(xb, w_ref[:, _GM0:_GM0 + D_MODEL]))
           * _sigmoid(_dot(xb, w_ref[:, _O0:_O0 + M_WIDTH])) * hg_ref[...])
    ti = lax.broadcasted_iota(jnp.int32, (MC, MC), 0)
    si = lax.broadcasted_iota(jnp.int32, (MC, MC), 1)
    causal = si <= ti
    for h in range(M_HEADS):
        sl = slice(h * M_HEAD_DIM, (h + 1) * M_HEAD_DIM)
        q, k, v = qa[:, sl], ka[:, sl], va[:, sl]
        negmx = cols[:, h:h + 1]
        w_int = cols[:, SUBLANES + h:SUBLANES + h + 1]
        en = cols[:, 2 * SUBLANES + h:2 * SUBLANES + h + 1]
        wk = cols[:, 3 * SUBLANES + h:3 * SUBLANES + h + 1]
        r = r_rows[h:h + 1, :]
        decay = decay_col[h:h + 1, :]

        s = lax.dot_general(q, k, (((1,), (1,)), ((), ())), preferred_element_type=F32)
        p = s * jnp.exp(jnp.where(causal, negmx + r, -jnp.inf))
        ct = ct_ref[h]
        n = n_ref[h:h + 1, :]
        num = w_int * _dot(q, ct.astype(BF16)) + _dot(p.astype(BF16), v)
        qn = jnp.sum(q.astype(F32) * n, axis=-1, keepdims=True)
        den = w_int * qn + jnp.sum(p, axis=-1, keepdims=True)
        hh = num / jnp.maximum(jnp.abs(den), en)
        mg_ref[:, sl] += gmo[:, sl] * _layernorm(hh, LN_EPS)

        kw = k.astype(F32) * wk
        ct_ref[h] = decay * ct + _dot(kw.T.astype(BF16), v)
        n_ref[h:h + 1, :] = decay * n + jnp.sum(kw, axis=0, keepdims=True)

    o_ref[...] = x + _dot(mg_ref[...].astype(BF16), wo_ref[...])


def _mixer(x1, g, w_pack, conv_w, conv_b, gate_b, head_g, sgu_g, sgu_b, w_s, b_s_t, w_out, *, bsz, seq):
    t = bsz * seq
    nc = seq // MC
    row = lambda b, c: (b * nc + c, 0)
    c2 = lambda b, c: (0, 0)
    c3 = lambda b, c: (0, 0, 0)
    return pl.pallas_call(
        _mixer_kernel,
        grid=(bsz, nc),
        in_specs=[
            pl.BlockSpec((MC, D_MODEL), row),
            pl.BlockSpec((1, D_MODEL), c2),
            pl.BlockSpec((D_MODEL, N_PACK), c2, pipeline_mode=pl.Buffered(1)),
            pl.BlockSpec((CONV_K, 2 * M_WIDTH), c2),
            pl.BlockSpec((1, 2 * M_WIDTH), c2),
            pl.BlockSpec((SUBLANES, 1), c2),
            pl.BlockSpec((1, M_WIDTH), c2),
            pl.BlockSpec((1, SGU_WIDTH), c2),
            pl.BlockSpec((1, SGU_WIDTH), c2),
            pl.BlockSpec((SGU_GROUPS, SGU_BLOCK, SGU_BLOCK), c3),
            pl.BlockSpec((SGU_BLOCK, SGU_GROUPS), c2),
            pl.BlockSpec((D_MODEL, D_MODEL), c2, pipeline_mode=pl.Buffered(1)),
        ],
        out_specs=pl.BlockSpec((MC, D_MODEL), row),
        out_shape=jax.ShapeDtypeStruct((t, D_MODEL), F32),
        scratch_shapes=[
            pltpu.VMEM((MC + 2 * SUBLANES, 2 * M_WIDTH), F32),
            pltpu.VMEM((M_HEADS, M_HEAD_DIM, M_HEAD_DIM), F32),
            pltpu.VMEM((SUBLANES, M_HEAD_DIM), F32),
            pltpu.VMEM((SUBLANES, LANES), F32),
            pltpu.VMEM((MC, D_MODEL), F32),
        ],
        compiler_params=pltpu.CompilerParams(
            dimension_semantics=("arbitrary", "arbitrary"), vmem_limit_bytes=VMEM_LIMIT),
        name="mixer",
    )(x1, g, w_pack, conv_w, conv_b, gate_b, head_g, sgu_g, sgu_b, w_s, b_s_t, w_out)


def _pack_proj_weight(w):
    o_end = 4 * M_WIDTH
    z0 = o_end + 2 * M_HEADS
    gate_cols = jnp.pad(w[:, o_end:z0], ((0, 0), (0, GATE_PAD - 2 * M_HEADS)))
    return jnp.concatenate([w[:, :o_end], w[:, z0:], gate_cols], axis=1).astype(BF16)


def kernel(x, ffn1_norm_g, ffn1_w_in, ffn1_w_out, mix_norm_g, w_mix_in, mlstm_conv_w, mlstm_conv_b, mlstm_b_igate, mlstm_b_fgate, mlstm_head_norm_g, sgu_norm_g, sgu_norm_b, sgu_w_s, sgu_b_s, w_mix_out, ffn2_norm_g, ffn2_w_in, ffn2_w_out, final_norm_g):
    bsz, seq, _ = x.shape
    h = x.reshape(bsz * seq, D_MODEL)
    fg = final_norm_g.reshape(1, D_MODEL)
    for l in range(ffn1_w_in.shape[0]):
        h = _ffn(h, ffn1_norm_g[l].reshape(1, -1), ffn1_w_in[l].astype(BF16), ffn1_w_out[l].astype(BF16),
                 fg, final_norm=False)

        gate_b = jnp.concatenate([mlstm_b_igate[l], mlstm_b_fgate[l]]).reshape(2 * M_HEADS, 1)
        h = _mixer(
            h, mix_norm_g[l].reshape(1, -1), _pack_proj_weight(w_mix_in[l]),
            mlstm_conv_w[l], mlstm_conv_b[l].reshape(1, -1), gate_b, mlstm_head_norm_g[l].reshape(1, -1),
            sgu_norm_g[l].reshape(1, -1), sgu_norm_b[l].reshape(1, -1), sgu_w_s[l], sgu_b_s[l].T,
            w_mix_out[l].astype(BF16), bsz=bsz, seq=seq)

        last = l == ffn1_w_in.shape[0] - 1
        h = _ffn(h, ffn2_norm_g[l].reshape(1, -1), ffn2_w_in[l].astype(BF16), ffn2_w_out[l].astype(BF16),
                 fg, final_norm=last)
    return h.reshape(bsz, seq, D_MODEL)
```

```python
import functools

import jax
import jax.numpy as jnp
from jax import lax
from jax.experimental import pallas as pl
from jax.experimental.pallas import tpu as pltpu

D_MODEL = 1024
CHUNK = 64
M_HEADS = 4
M_HEAD_DIM = D_MODEL // M_HEADS
M_WIDTH = M_HEADS * M_HEAD_DIM
CONV_K = 4
SGU_BLOCK = 128
SGU_GROUPS = 4
SGU_WIDTH = D_MODEL
SGU_GROUP_DIM = SGU_WIDTH // SGU_GROUPS
D_FF = 2816
RMS_EPS = 1e-6
LN_EPS = 1e-5

MC = 512
TM = 512
SCAN_LAG = 3
GATE_PAD = 128
SUBLANES = 8
LANES = 128
FF_TILE = 256
VMEM_LIMIT = 56 * 1024 * 1024

F32 = jnp.float32
BF16 = jnp.bfloat16


def _dot(a, b):
    return jnp.dot(a, b, preferred_element_type=F32)


def _rmsnorm(x, g):
    return x * lax.rsqrt(jnp.mean(x * x, axis=-1, keepdims=True) + RMS_EPS) * g


def _layernorm(x, eps):
    xc = x - jnp.mean(x, axis=-1, keepdims=True)
    return xc * lax.rsqrt(jnp.mean(xc * xc, axis=-1, keepdims=True) + eps)


def _sigmoid(x):
    return 1.0 / (1.0 + jnp.exp(-x))


def _gelu(z):
    return 0.5 * z * (1.0 + lax.erf(z * (2.0 ** -0.5)))


def _ffn_kernel(x_ref, g_ref, win_ref, wout_ref, fg_ref, o_ref, h_ref, *, final_norm):
    x = x_ref[...]
    xb = _rmsnorm(x, g_ref[...]).astype(BF16)
    for j in range(D_FF // FF_TILE):
        lo = j * FF_TILE
        gate = _dot(xb, win_ref[:, lo:lo + FF_TILE])
        up = _dot(xb, win_ref[:, D_FF + lo:D_FF + lo + FF_TILE])
        h_ref[:, lo:lo + FF_TILE] = (gate * _sigmoid(gate) * up).astype(BF16)
    y = x + 0.5 * _dot(h_ref[...], wout_ref[...])
    if final_norm:
        y = _rmsnorm(y, fg_ref[...])
    o_ref[...] = y


def _ffn(x2d, g, w_in, w_out, final_g, *, final_norm, tm=512):
    t = x2d.shape[0]
    const = lambda i: (0, 0)
    return pl.pallas_call(
        functools.partial(_ffn_kernel, final_norm=final_norm),
        grid=(t // tm,),
        in_specs=[
            pl.BlockSpec((tm, D_MODEL), lambda i: (i, 0)),
            pl.BlockSpec((1, D_MODEL), const),
            pl.BlockSpec((D_MODEL, 2 * D_FF), const, pipeline_mode=pl.Buffered(1)),
            pl.BlockSpec((D_FF, D_MODEL), const, pipeline_mode=pl.Buffered(1)),
            pl.BlockSpec((1, D_MODEL), const),
        ],
        out_specs=pl.BlockSpec((tm, D_MODEL), lambda i: (i, 0)),
        out_shape=jax.ShapeDtypeStruct((t, D_MODEL), F32),
        scratch_shapes=[pltpu.VMEM((tm, D_FF), BF16)],
        compiler_params=pltpu.CompilerParams(
            dimension_semantics=("parallel",), vmem_limit_bytes=VMEM_LIMIT),
        name="ffn_final" if final_norm else "ffn",
    )(x2d, g, w_in, w_out, final_g)


_QK0, _V0, _O0, _Z0, _GM0, _GS0, _IF0 = 0, 2048, 3072, 4096, 6144, 7168, 8192
N_PACK = 8192 + GATE_PAD


def _prefix_scan(x, op, lane):
    n = x.shape[-1]
    sh = 1
    while sh < n:
        x = jnp.where(lane >= sh, op(x, pltpu.roll(x, sh, axis=1)), x)
        sh *= 2
    return x


def _gate_vectors(gif, gb, m_ref):
    row = lax.broadcasted_iota(jnp.int32, (SUBLANES, MC), 0)
    lane = lax.broadcasted_iota(jnp.int32, (SUBLANES, MC), 1)
    head_rows = row < M_HEADS
    g8 = gif.T[0:SUBLANES, :] + gb
    lsig = jnp.minimum(g8, 0.0) - jnp.log1p(jnp.exp(-jnp.abs(g8)))
    lf = jnp.where(head_rows, pltpu.roll(lsig, M_HEADS, axis=0), 0.0)
    li = jnp.where(head_rows, g8, 0.0)
    b = _prefix_scan(lf, jnp.add, lane)
    r = li - b
    cm = _prefix_scan(r, jnp.maximum, lane)
    m_in = m_ref[:, 0:1]
    mx = jnp.maximum(m_in, cm)
    mx_last = mx[:, MC - 1:MC]
    m_ref[...] = jnp.broadcast_to(b[:, MC - 1:MC] + mx_last, m_ref.shape)
    cols = jnp.concatenate(
        [-mx, jnp.exp(m_in - mx), jnp.exp(-(b + mx)), jnp.exp(r - mx_last),
         jnp.zeros((LANES - 4 * SUBLANES, MC), F32)], axis=0).T
    return cols, r, jnp.exp(m_in - mx_last)


def _mixer_kernel(x_ref, g_ref, w_ref, cw_ref, cb_ref, gb_ref, hg_ref, sg_ref, sb_ref, ws_ref, bst_ref, wo_ref,
                  o_ref, buf_ref, ct_ref, n_ref, m_ref, qa_ref, ka_ref, va_ref, gmo_ref, sgu_ref, gu_ref, zv_ref,
                  vn_ref, xb_ref, *, tiles_per_seq):
    @pl.when(lax.rem(pl.program_id(0), tiles_per_seq) == 0)
    def _():
        buf_ref[0:SUBLANES, :] = jnp.zeros((SUBLANES, 2 * M_WIDTH), F32)
        ct_ref[...] = jnp.zeros(ct_ref.shape, F32)
        n_ref[...] = jnp.zeros(n_ref.shape, F32)
        m_ref[...] = jnp.zeros(m_ref.shape, F32)

    n_chunks = TM // MC
    xb_ref[...] = _rmsnorm(x_ref[...], g_ref[...]).astype(BF16)

    def head_cols(h):
        return slice(h * M_HEAD_DIM, (h + 1) * M_HEAD_DIM)

    def proj(lo, n=M_HEAD_DIM):
        return _dot(xb_ref[...], w_ref[:, lo:lo + n])

    gif = proj(_IF0, GATE_PAD)
    gates = [_gate_vectors(gif[c * MC:(c + 1) * MC, :], gb_ref[...], m_ref) for c in range(n_chunks)]
    ti = lax.broadcasted_iota(jnp.int32, (MC, MC), 0)
    si = lax.broadcasted_iota(jnp.int32, (MC, MC), 1)
    causal = si <= ti

    def scan_unit(c, h):
        ts = slice(c * MC, (c + 1) * MC)
        sl = head_cols(h)
        cols, r_rows, decay_col = gates[c]
        q, k, v = qa_ref[ts, sl], ka_ref[ts, sl], va_ref[ts, sl]
        negmx = cols[:, h:h + 1]
        w_int = cols[:, SUBLANES + h:SUBLANES + h + 1]
        en = cols[:, 2 * SUBLANES + h:2 * SUBLANES + h + 1]
        wk = cols[:, 3 * SUBLANES + h:3 * SUBLANES + h + 1]
        r = r_rows[h:h + 1, :]
        decay = decay_col[h:h + 1, :]

        s = lax.dot_general(q, k, (((1,), (1,)), ((), ())), preferred_element_type=F32)
        ct = ct_ref[h]
        n = n_ref[h:h + 1, :]
        inter = _dot(q, ct.astype(BF16))
        kw = k.astype(F32) * wk
        ct_ref[h] = decay * ct + _dot(kw.T.astype(BF16), v)
        n_ref[h:h + 1, :] = decay * n + jnp.sum(kw, axis=0, keepdims=True)
        yield
        p = s * jnp.exp(jnp.where(causal, negmx + r, -jnp.inf))
        num = w_int * inter + _dot(p.astype(BF16), v)
        qn = jnp.sum(q.astype(F32) * n, axis=-1, keepdims=True)
        den = w_int * qn + jnp.sum(p, axis=-1, keepdims=True)
        hh = num / jnp.maximum(jnp.abs(den), en)
        yield
        merged = gmo_ref[ts, sl] * _layernorm(hh, LN_EPS) + sgu_ref[ts, sl]
        part = _dot(merged.astype(BF16), wo_ref[sl, :])
        o_ref[ts, :] = (x_ref[ts, :] if h == 0 else o_ref[ts, :]) + part
        yield

    def piece_u_gate(j):
        cs = head_cols(j)
        gu_ref[:, cs] = _sigmoid(proj(_GS0 + cs.start)) * _gelu(proj(_Z0 + cs.start))

    def piece_v_act(j):
        cs = head_cols(j)
        zv_ref[:, cs] = _gelu(proj(_Z0 + SGU_WIDTH + cs.start))

    def piece_v_norm():
        vn_ref[...] = (_layernorm(zv_ref[...], LN_EPS) * sg_ref[...] + sb_ref[...]).astype(BF16)

    def piece_spatial(g):
        bi = lax.broadcasted_iota(jnp.int32, (SGU_BLOCK, SGU_BLOCK), 0) // CHUNK
        bj = lax.broadcasted_iota(jnp.int32, (SGU_BLOCK, SGU_BLOCK), 1) // CHUNK
        wg = jnp.where(bi >= bj, ws_ref[g], 0.0).astype(BF16)
        bg = bst_ref[:, g:g + 1]
        cs = head_cols(g)
        for nb in range(TM // SGU_BLOCK):
            rs = slice(nb * SGU_BLOCK, (nb + 1) * SGU_BLOCK)
            sgu_ref[rs, cs] = gu_ref[rs, cs] * (_dot(wg, vn_ref[rs, cs]) + bg)

    def piece_conv(h, base, dst, scale):
        cs = slice(base + h * M_HEAD_DIM, base + (h + 1) * M_HEAD_DIM)
        raw = proj(_QK0 + cs.start)
        buf_ref[SUBLANES:SUBLANES + TM, cs] = raw
        acc = raw * cw_ref[CONV_K - 1:CONV_K, cs] + cb_ref[:, cs]
        for j in range(CONV_K - 1):
            off = SUBLANES - (CONV_K - 1) + j
            acc = acc + buf_ref[off:off + TM, cs] * cw_ref[j:j + 1, cs]
        buf_ref[0:SUBLANES, cs] = buf_ref[TM:TM + SUBLANES, cs]
        dst[:, head_cols(h)] = (acc * _sigmoid(acc) * scale).astype(BF16)

    def piece_v(h):
        sl = head_cols(h)
        va_ref[:, sl] = proj(_V0 + sl.start).astype(BF16)

    def piece_gmo(h):
        sl = head_cols(h)
        gmo_ref[:, sl] = _sigmoid(proj(_GM0 + sl.start)) * _sigmoid(proj(_O0 + sl.start)) * hg_ref[:, sl]

    bind = functools.partial
    units = [(c, h) for h in range(M_HEADS) for c in range(n_chunks)]
    phases = 3
    order = [(t - p, p) for t in range(len(units) + phases - 1) for p in range(phases) if 0 <= t - p < len(units)]
    position = {up: b for b, up in enumerate(order)}
    queue = []
    for h in range(M_HEADS):
        first = max(position[(n_chunks * h, 0)] - SCAN_LAG, 0)
        final = max(position[(n_chunks * h, phases - 1)] - SCAN_LAG, 0)
        queue += [(first, bind(piece_conv, h, 0, qa_ref, 1.0)),
                  (first, bind(piece_conv, h, M_WIDTH, ka_ref, M_HEAD_DIM ** -0.5)),
                  (first, bind(piece_v, h))]
        if h == 0:
            queue += [(final, bind(piece_v_act, j)) for j in range(SGU_GROUPS)]
            queue += [(final, piece_v_norm)]
        queue += [(final, bind(piece_u_gate, h)), (final, bind(piece_spatial, h)), (final, bind(piece_gmo, h))]
    queue.sort(key=lambda item: item[0])

    total = len(queue)
    last = queue[-1][0]

    def emit(boundary):
        want = -(-total * (boundary + 1) // (last + 1))
        while queue and (queue[0][0] <= boundary or total - len(queue) < want):
            queue.pop(0)[1]()

    emit(0)
    running = [scan_unit(c, h) for c, h in units]
    for boundary, (u, _) in enumerate(order, start=1):
        next(running[u])
        emit(boundary)
    assert not queue


def _mixer(x1, g, w_pack, conv_w, conv_b, gate_b, head_g, sgu_g, sgu_b, w_s, b_s_t, w_out, *, bsz, seq):
    t = bsz * seq
    row = lambda i: (i, 0)
    c2 = lambda i: (0, 0)
    c3 = lambda i: (0, 0, 0)
    return pl.pallas_call(
        functools.partial(_mixer_kernel, tiles_per_seq=seq // TM),
        grid=(t // TM,),
        in_specs=[
            pl.BlockSpec((TM, D_MODEL), row),
            pl.BlockSpec((1, D_MODEL), c2),
            pl.BlockSpec((D_MODEL, N_PACK), c2, pipeline_mode=pl.Buffered(1)),
            pl.BlockSpec((CONV_K, 2 * M_WIDTH), c2),
            pl.BlockSpec((1, 2 * M_WIDTH), c2),
            pl.BlockSpec((SUBLANES, 1), c2),
            pl.BlockSpec((1, M_WIDTH), c2),
            pl.BlockSpec((1, SGU_WIDTH), c2),
            pl.BlockSpec((1, SGU_WIDTH), c2),
            pl.BlockSpec((SGU_GROUPS, SGU_BLOCK, SGU_BLOCK), c3),
            pl.BlockSpec((SGU_BLOCK, SGU_GROUPS), c2),
            pl.BlockSpec((D_MODEL, D_MODEL), c2, pipeline_mode=pl.Buffered(1)),
        ],
        out_specs=pl.BlockSpec((TM, D_MODEL), row),
        out_shape=jax.ShapeDtypeStruct((t, D_MODEL), F32),
        scratch_shapes=[
            pltpu.VMEM((TM + 2 * SUBLANES, 2 * M_WIDTH), F32),
            pltpu.VMEM((M_HEADS, M_HEAD_DIM, M_HEAD_DIM), F32),
            pltpu.VMEM((SUBLANES, M_HEAD_DIM), F32),
            pltpu.VMEM((SUBLANES, LANES), F32),
            pltpu.VMEM((TM, M_WIDTH), BF16),
            pltpu.VMEM((TM, M_WIDTH), BF16),
            pltpu.VMEM((TM, M_WIDTH), BF16),
            pltpu.VMEM((TM, M_WIDTH), F32),
            pltpu.VMEM((TM, SGU_WIDTH), F32),
            pltpu.VMEM((TM, SGU_WIDTH), F32),
            pltpu.VMEM((TM, SGU_WIDTH), F32),
            pltpu.VMEM((TM, SGU_WIDTH), BF16),
            pltpu.VMEM((TM, D_MODEL), BF16),
        ],
        compiler_params=pltpu.CompilerParams(
            dimension_semantics=("arbitrary",), vmem_limit_bytes=VMEM_LIMIT),
        name="mixer",
    )(x1, g, w_pack, conv_w, conv_b, gate_b, head_g, sgu_g, sgu_b, w_s, b_s_t, w_out)


def _pack_proj_weight(w):
    o_end = 4 * M_WIDTH
    z0 = o_end + 2 * M_HEADS
    gate_cols = jnp.pad(w[:, o_end:z0], ((0, 0), (0, GATE_PAD - 2 * M_HEADS)))
    return jnp.concatenate([w[:, :o_end], w[:, z0:], gate_cols], axis=1).astype(BF16)


def kernel(x, ffn1_norm_g, ffn1_w_in, ffn1_w_out, mix_norm_g, w_mix_in, mlstm_conv_w, mlstm_conv_b, mlstm_b_igate, mlstm_b_fgate, mlstm_head_norm_g, sgu_norm_g, sgu_norm_b, sgu_w_s, sgu_b_s, w_mix_out, ffn2_norm_g, ffn2_w_in, ffn2_w_out, final_norm_g):
    bsz, seq, _ = x.shape
    h = x.reshape(bsz * seq, D_MODEL)
    fg = final_norm_g.reshape(1, D_MODEL)
    for l in range(ffn1_w_in.shape[0]):
        h = _ffn(h, ffn1_norm_g[l].reshape(1, -1), ffn1_w_in[l].astype(BF16), ffn1_w_out[l].astype(BF16),
                 fg, final_norm=False)

        gate_b = jnp.concatenate([mlstm_b_igate[l], mlstm_b_fgate[l]]).reshape(2 * M_HEADS, 1)
        h = _mixer(
            h, mix_norm_g[l].reshape(1, -1), _pack_proj_weight(w_mix_in[l]),
            mlstm_conv_w[l], mlstm_conv_b[l].reshape(1, -1), gate_b, mlstm_head_norm_g[l].reshape(1, -1),
            sgu_norm_g[l].reshape(1, -1), sgu_norm_b[l].reshape(1, -1), sgu_w_s[l], sgu_b_s[l].T,
            w_mix_out[l].astype(BF16), bsz=bsz, seq=seq)

        last = l == ffn1_w_in.shape[0] - 1
        h = _ffn(h, ffn2_norm_g[l].reshape(1, -1), ffn2_w_in[l].astype(BF16), ffn2_w_out[l].astype(BF16),
                 fg, final_norm=last)
    return h.reshape(bsz, seq, D_MODEL)
```

```python
import functools

import jax
import jax.numpy as jnp
from jax import lax
from jax.experimental import pallas as pl
from jax.experimental.pallas import tpu as pltpu

D_MODEL = 1024
CHUNK = 64
M_HEADS = 4
M_HEAD_DIM = D_MODEL // M_HEADS
M_WIDTH = M_HEADS * M_HEAD_DIM
CONV_K = 4
SGU_BLOCK = 128
SGU_GROUPS = 4
SGU_WIDTH = D_MODEL
SGU_GROUP_DIM = SGU_WIDTH // SGU_GROUPS
D_FF = 2816
RMS_EPS = 1e-6
LN_EPS = 1e-5

MC = 512
TM = 512
SCAN_LAG = 3
GATE_PAD = 128
SUBLANES = 8
BF16_SUBLANES = 16
LANES = 128
FF_TILE = 256
VMEM_LIMIT = 56 * 1024 * 1024

F32 = jnp.float32
BF16 = jnp.bfloat16


def _dot(a, b):
    return jnp.dot(a, b, preferred_element_type=F32)


def _rmsnorm(x, g):
    return x * lax.rsqrt(jnp.mean(x * x, axis=-1, keepdims=True) + RMS_EPS) * g


def _layernorm(x, eps):
    xc = x - jnp.mean(x, axis=-1, keepdims=True)
    return xc * lax.rsqrt(jnp.mean(xc * xc, axis=-1, keepdims=True) + eps)


def _sigmoid(x):
    return 1.0 / (1.0 + jnp.exp(-x))


def _gelu(z):
    return 0.5 * z * (1.0 + lax.erf(z * (2.0 ** -0.5)))


def _ffn_body(x_ref, g_ref, win_ref, wout_ref, fg_ref, o_ref, h_ref, final_norm):
    x = x_ref[...]
    xb = _rmsnorm(x, g_ref[...]).astype(BF16)
    for j in range(D_FF // FF_TILE):
        lo = j * FF_TILE
        gate = _dot(xb, win_ref[:, lo:lo + FF_TILE])
        up = _dot(xb, win_ref[:, D_FF + lo:D_FF + lo + FF_TILE])
        h_ref[:, lo:lo + FF_TILE] = (gate * _sigmoid(gate) * up).astype(BF16)
    y = x + 0.5 * _dot(h_ref[...], wout_ref[...])
    if final_norm:
        y = _rmsnorm(y, fg_ref[...])
    o_ref[...] = y


def _ffn_kernel(x_ref, g_ref, win_ref, wout_ref, fg_ref, o_ref, h_ref, *, final_norm):
    _ffn_body(x_ref, g_ref, win_ref, wout_ref, fg_ref, o_ref, h_ref, final_norm)


def _ffn_cast_kernel(x_ref, g_ref, win_ref, wout_ref, fg_ref, wmix_ref, w2in_ref, w2out_ref, wo_ref,
                     o_ref, wa_ref, wb_ref, w2in_bf_ref, w2out_bf_ref, wo_bf_ref, h_ref, *, w2out_steps):
    _ffn_body(x_ref, g_ref, win_ref, wout_ref, fg_ref, o_ref, h_ref, False)
    wa_ref[...] = wmix_ref[:, 0:W_HEAD].astype(BF16)
    wb_ref[...] = wmix_ref[:, W_HEAD + W_GATES:W_HEAD + W_GATES + W_TAIL].astype(BF16)
    w2in_bf_ref[...] = w2in_ref[...].astype(BF16)
    wo_bf_ref[...] = wo_ref[...].astype(BF16)

    @pl.when(pl.program_id(0) < w2out_steps)
    def _():
        w2out_bf_ref[...] = w2out_ref[...].astype(BF16)


def _ffn_specs(tm):
    const = lambda i: (0, 0)
    return [
        pl.BlockSpec((tm, D_MODEL), lambda i: (i, 0)),
        pl.BlockSpec((1, D_MODEL), const),
        pl.BlockSpec((D_MODEL, 2 * D_FF), const, pipeline_mode=pl.Buffered(1)),
        pl.BlockSpec((D_FF, D_MODEL), const, pipeline_mode=pl.Buffered(1)),
        pl.BlockSpec((1, D_MODEL), const),
    ]


def _ffn(x2d, g, w_in, w_out, final_g, *, final_norm, tm=512):
    t = x2d.shape[0]
    return pl.pallas_call(
        functools.partial(_ffn_kernel, final_norm=final_norm),
        grid=(t // tm,),
        in_specs=_ffn_specs(tm),
        out_specs=pl.BlockSpec((tm, D_MODEL), lambda i: (i, 0)),
        out_shape=jax.ShapeDtypeStruct((t, D_MODEL), F32),
        scratch_shapes=[pltpu.VMEM((tm, D_FF), BF16)],
        compiler_params=pltpu.CompilerParams(
            dimension_semantics=("parallel",), vmem_limit_bytes=VMEM_LIMIT),
        name="ffn_final" if final_norm else "ffn",
    )(x2d, g, w_in, w_out, final_g)


def _ffn_cast(x2d, g, w_in, w_out, final_g, w_mix, w2_in, w2_out, w_o, *, tm=512):
    t = x2d.shape[0]
    steps = t // tm
    rows = D_MODEL // steps
    w2out_steps = steps // 2
    w2out_rows = D_FF // w2out_steps
    assert rows % BF16_SUBLANES == 0 and w2out_rows % BF16_SUBLANES == 0
    slab = lambda i: (i, 0)
    slab2 = lambda i: (jnp.minimum(i, w2out_steps - 1), 0)
    bf = lambda shape: jax.ShapeDtypeStruct(shape, BF16)
    return pl.pallas_call(
        functools.partial(_ffn_cast_kernel, w2out_steps=w2out_steps),
        grid=(steps,),
        in_specs=_ffn_specs(tm) + [
            pl.BlockSpec((rows, w_mix.shape[1]), slab),
            pl.BlockSpec((rows, 2 * D_FF), slab),
            pl.BlockSpec((w2out_rows, D_MODEL), slab2),
            pl.BlockSpec((rows, D_MODEL), slab),
        ],
        out_specs=[
            pl.BlockSpec((tm, D_MODEL), lambda i: (i, 0)),
            pl.BlockSpec((rows, W_HEAD), slab),
            pl.BlockSpec((rows, W_TAIL), slab),
            pl.BlockSpec((rows, 2 * D_FF), slab),
            pl.BlockSpec((w2out_rows, D_MODEL), slab2),
            pl.BlockSpec((rows, D_MODEL), slab),
        ],
        out_shape=[jax.ShapeDtypeStruct((t, D_MODEL), F32), bf((D_MODEL, W_HEAD)), bf((D_MODEL, W_TAIL)),
                   bf((D_MODEL, 2 * D_FF)), bf((D_FF, D_MODEL)), bf((D_MODEL, D_MODEL))],
        scratch_shapes=[pltpu.VMEM((tm, D_FF), BF16)],
        compiler_params=pltpu.CompilerParams(
            dimension_semantics=("arbitrary",), vmem_limit_bytes=VMEM_LIMIT),
        name="ffn_cast",
    )(x2d, g, w_in, w_out, final_g, w_mix, w2_in, w2_out, w_o)


W_HEAD = 4 * M_WIDTH
W_GATES = 2 * M_HEADS
W_TAIL = 2 * SGU_WIDTH + 2 * D_MODEL
_QK0, _V0, _O0 = 0, 2 * M_WIDTH, 3 * M_WIDTH
_Z0, _GM0, _GS0 = 0, 2 * SGU_WIDTH, 2 * SGU_WIDTH + D_MODEL


def _prefix_scan(x, op, lane):
    n = x.shape[-1]
    sh = 1
    while sh < n:
        x = jnp.where(lane >= sh, op(x, pltpu.roll(x, sh, axis=1)), x)
        sh *= 2
    return x


def _gate_vectors(gif, gb, m_ref):
    row = lax.broadcasted_iota(jnp.int32, (SUBLANES, MC), 0)
    lane = lax.broadcasted_iota(jnp.int32, (SUBLANES, MC), 1)
    head_rows = row < M_HEADS
    g8 = gif.T[0:SUBLANES, :] + gb
    lsig = jnp.minimum(g8, 0.0) - jnp.log1p(jnp.exp(-jnp.abs(g8)))
    lf = jnp.where(head_rows, pltpu.roll(lsig, M_HEADS, axis=0), 0.0)
    li = jnp.where(head_rows, g8, 0.0)
    b = _prefix_scan(lf, jnp.add, lane)
    r = li - b
    cm = _prefix_scan(r, jnp.maximum, lane)
    m_in = m_ref[:, 0:1]
    mx = jnp.maximum(m_in, cm)
    mx_last = mx[:, MC - 1:MC]
    m_ref[...] = jnp.broadcast_to(b[:, MC - 1:MC] + mx_last, m_ref.shape)
    cols = jnp.concatenate(
        [-mx, jnp.exp(m_in - mx), jnp.exp(-(b + mx)), jnp.exp(r - mx_last),
         jnp.zeros((LANES - 4 * SUBLANES, MC), F32)], axis=0).T
    return cols, r, jnp.exp(m_in - mx_last)


def _mixer_kernel(x_ref, g_ref, wa_ref, wb_ref, wif_ref, cw_ref, cb_ref, gb_ref, hg_ref, sg_ref, sb_ref, ws_ref,
                  bst_ref, wo_ref,
                  o_ref, buf_ref, ct_ref, n_ref, m_ref, qa_ref, ka_ref, va_ref, gmo_ref, sgu_ref, gu_ref, zv_ref,
                  vn_ref, xb_ref, *, tiles_per_seq):
    @pl.when(lax.rem(pl.program_id(0), tiles_per_seq) == 0)
    def _():
        buf_ref[0:SUBLANES, :] = jnp.zeros((SUBLANES, 2 * M_WIDTH), F32)
        ct_ref[...] = jnp.zeros(ct_ref.shape, F32)
        n_ref[...] = jnp.zeros(n_ref.shape, F32)
        m_ref[...] = jnp.zeros(m_ref.shape, F32)

    n_chunks = TM // MC
    xb_ref[...] = _rmsnorm(x_ref[...], g_ref[...]).astype(BF16)

    def head_cols(h):
        return slice(h * M_HEAD_DIM, (h + 1) * M_HEAD_DIM)

    def proj_head(lo):
        return _dot(xb_ref[...], wa_ref[:, lo:lo + M_HEAD_DIM])

    def proj_tail(lo):
        return _dot(xb_ref[...], wb_ref[:, lo:lo + M_HEAD_DIM])

    gif = _dot(xb_ref[...], wif_ref[...])
    gates = [_gate_vectors(gif[c * MC:(c + 1) * MC, :], gb_ref[...], m_ref) for c in range(n_chunks)]
    ti = lax.broadcasted_iota(jnp.int32, (MC, MC), 0)
    si = lax.broadcasted_iota(jnp.int32, (MC, MC), 1)
    causal = si <= ti

    def scan_unit(c, h):
        ts = slice(c * MC, (c + 1) * MC)
        sl = head_cols(h)
        cols, r_rows, decay_col = gates[c]
        q, k, v = qa_ref[ts, sl], ka_ref[ts, sl], va_ref[ts, sl]
        negmx = cols[:, h:h + 1]
        w_int = cols[:, SUBLANES + h:SUBLANES + h + 1]
        en = cols[:, 2 * SUBLANES + h:2 * SUBLANES + h + 1]
        wk = cols[:, 3 * SUBLANES + h:3 * SUBLANES + h + 1]
        r = r_rows[h:h + 1, :]
        decay = decay_col[h:h + 1, :]

        s = lax.dot_general(q, k, (((1,), (1,)), ((), ())), preferred_element_type=F32)
        ct = ct_ref[h]
        n = n_ref[h:h + 1, :]
        inter = _dot(q, ct.astype(BF16))
        kw = k.astype(F32) * wk
        ct_ref[h] = decay * ct + _dot(kw.T.astype(BF16), v)
        n_ref[h:h + 1, :] = decay * n + jnp.sum(kw, axis=0, keepdims=True)
        yield
        p = s * jnp.exp(jnp.where(causal, negmx + r, -jnp.inf))
        num = w_int * inter + _dot(p.astype(BF16), v)
        qn = jnp.sum(q.astype(F32) * n, axis=-1, keepdims=True)
        den = w_int * qn + jnp.sum(p, axis=-1, keepdims=True)
        hh = num / jnp.maximum(jnp.abs(den), en)
        yield
        merged = gmo_ref[ts, sl] * _layernorm(hh, LN_EPS) + sgu_ref[ts, sl]
        part = _dot(merged.astype(BF16), wo_ref[sl, :])
        o_ref[ts, :] = (x_ref[ts, :] if h == 0 else o_ref[ts, :]) + part
        yield

    def piece_u_gate(j):
        cs = head_cols(j)
        gu_ref[:, cs] = _sigmoid(proj_tail(_GS0 + cs.start)) * _gelu(proj_tail(_Z0 + cs.start))

    def piece_v_act(j):
        cs = head_cols(j)
        zv_ref[:, cs] = _gelu(proj_tail(_Z0 + SGU_WIDTH + cs.start))

    def piece_v_norm():
        vn_ref[...] = (_layernorm(zv_ref[...], LN_EPS) * sg_ref[...] + sb_ref[...]).astype(BF16)

    def piece_spatial(g):
        bi = lax.broadcasted_iota(jnp.int32, (SGU_BLOCK, SGU_BLOCK), 0) // CHUNK
        bj = lax.broadcasted_iota(jnp.int32, (SGU_BLOCK, SGU_BLOCK), 1) // CHUNK
        wg = jnp.where(bi >= bj, ws_ref[g], 0.0).astype(BF16)
        bg = bst_ref[:, g:g + 1]
        cs = head_cols(g)
        for nb in range(TM // SGU_BLOCK):
            rs = slice(nb * SGU_BLOCK, (nb + 1) * SGU_BLOCK)
            sgu_ref[rs, cs] = gu_ref[rs, cs] * (_dot(wg, vn_ref[rs, cs]) + bg)

    def piece_conv(h, base, dst, scale):
        cs = slice(base + h * M_HEAD_DIM, base + (h + 1) * M_HEAD_DIM)
        raw = proj_head(_QK0 + cs.start)
        buf_ref[SUBLANES:SUBLANES + TM, cs] = raw
        acc = raw * cw_ref[CONV_K - 1:CONV_K, cs] + cb_ref[:, cs]
        for j in range(CONV_K - 1):
            off = SUBLANES - (CONV_K - 1) + j
            acc = acc + buf_ref[off:off + TM, cs] * cw_ref[j:j + 1, cs]
        buf_ref[0:SUBLANES, cs] = buf_ref[TM:TM + SUBLANES, cs]
        dst[:, head_cols(h)] = (acc * _sigmoid(acc) * scale).astype(BF16)

    def piece_v(h):
        sl = head_cols(h)
        va_ref[:, sl] = proj_head(_V0 + sl.start).astype(BF16)

    def piece_gmo(h):
        sl = head_cols(h)
        gmo_ref[:, sl] = (_sigmoid(proj_tail(_GM0 + sl.start)) * _sigmoid(proj_head(_O0 + sl.start))
                          * hg_ref[:, sl])

    bind = functools.partial
    units = [(c, h) for h in range(M_HEADS) for c in range(n_chunks)]
    phases = 3
    order = [(t - p, p) for t in range(len(units) + phases - 1) for p in range(phases) if 0 <= t - p < len(units)]
    position = {up: b for b, up in enumerate(order)}
    queue = []
    for h in range(M_HEADS):
        first = max(position[(n_chunks * h, 0)] - SCAN_LAG, 0)
        final = max(position[(n_chunks * h, phases - 1)] - SCAN_LAG, 0)
        queue += [(first, bind(piece_conv, h, 0, qa_ref, 1.0)),
                  (first, bind(piece_conv, h, M_WIDTH, ka_ref, M_HEAD_DIM ** -0.5)),
                  (first, bind(piece_v, h))]
        if h == 0:
            queue += [(final, bind(piece_v_act, j)) for j in range(SGU_GROUPS)]
            queue += [(final, piece_v_norm)]
        queue += [(final, bind(piece_u_gate, h)), (final, bind(piece_spatial, h)), (final, bind(piece_gmo, h))]
    queue.sort(key=lambda item: item[0])

    total = len(queue)
    last = queue[-1][0]

    def emit(boundary):
        want = -(-total * (boundary + 1) // (last + 1))
        while queue and (queue[0][0] <= boundary or total - len(queue) < want):
            queue.pop(0)[1]()

    emit(0)
    running = [scan_unit(c, h) for c, h in units]
    for boundary, (u, _) in enumerate(order, start=1):
        next(running[u])
        emit(boundary)
    assert not queue


def _mixer(x1, g, w_head, w_tail, w_gates, conv_w, conv_b, gate_b, head_g, sgu_g, sgu_b, w_s, b_s_t, w_out,
           *, bsz, seq):
    t = bsz * seq
    row = lambda i: (i, 0)
    c2 = lambda i: (0, 0)
    c3 = lambda i: (0, 0, 0)
    return pl.pallas_call(
        functools.partial(_mixer_kernel, tiles_per_seq=seq // TM),
        grid=(t // TM,),
        in_specs=[
            pl.BlockSpec((TM, D_MODEL), row),
            pl.BlockSpec((1, D_MODEL), c2),
            pl.BlockSpec((D_MODEL, W_HEAD), c2, pipeline_mode=pl.Buffered(1)),
            pl.BlockSpec((D_MODEL, W_TAIL), c2, pipeline_mode=pl.Buffered(1)),
            pl.BlockSpec((D_MODEL, GATE_PAD), c2),
            pl.BlockSpec((CONV_K, 2 * M_WIDTH), c2),
            pl.BlockSpec((1, 2 * M_WIDTH), c2),
            pl.BlockSpec((SUBLANES, 1), c2),
            pl.BlockSpec((1, M_WIDTH), c2),
            pl.BlockSpec((1, SGU_WIDTH), c2),
            pl.BlockSpec((1, SGU_WIDTH), c2),
            pl.BlockSpec((SGU_GROUPS, SGU_BLOCK, SGU_BLOCK), c3),
            pl.BlockSpec((SGU_BLOCK, SGU_GROUPS), c2),
            pl.BlockSpec((D_MODEL, D_MODEL), c2, pipeline_mode=pl.Buffered(1)),
        ],
        out_specs=pl.BlockSpec((TM, D_MODEL), row),
        out_shape=jax.ShapeDtypeStruct((t, D_MODEL), F32),
        scratch_shapes=[
            pltpu.VMEM((TM + 2 * SUBLANES, 2 * M_WIDTH), F32),
            pltpu.VMEM((M_HEADS, M_HEAD_DIM, M_HEAD_DIM), F32),
            pltpu.VMEM((SUBLANES, M_HEAD_DIM), F32),
            pltpu.VMEM((SUBLANES, LANES), F32),
            pltpu.VMEM((TM, M_WIDTH), BF16),
            pltpu.VMEM((TM, M_WIDTH), BF16),
            pltpu.VMEM((TM, M_WIDTH), BF16),
            pltpu.VMEM((TM, M_WIDTH), F32),
            pltpu.VMEM((TM, SGU_WIDTH), F32),
            pltpu.VMEM((TM, SGU_WIDTH), F32),
            pltpu.VMEM((TM, SGU_WIDTH), F32),
            pltpu.VMEM((TM, SGU_WIDTH), BF16),
            pltpu.VMEM((TM, D_MODEL), BF16),
        ],
        compiler_params=pltpu.CompilerParams(
            dimension_semantics=("arbitrary",), vmem_limit_bytes=VMEM_LIMIT),
        name="mixer",
    )(x1, g, w_head, w_tail, w_gates, conv_w, conv_b, gate_b, head_g, sgu_g, sgu_b, w_s, b_s_t, w_out)


def kernel(x, ffn1_norm_g, ffn1_w_in, ffn1_w_out, mix_norm_g, w_mix_in, mlstm_conv_w, mlstm_conv_b, mlstm_b_igate, mlstm_b_fgate, mlstm_head_norm_g, sgu_norm_g, sgu_norm_b, sgu_w_s, sgu_b_s, w_mix_out, ffn2_norm_g, ffn2_w_in, ffn2_w_out, final_norm_g):
    bsz, seq, _ = x.shape
    h = x.reshape(bsz * seq, D_MODEL)
    fg = final_norm_g.reshape(1, D_MODEL)
    for l in range(ffn1_w_in.shape[0]):
        h, w_head, w_tail, w2_in, w2_out, w_o = _ffn_cast(
            h, ffn1_norm_g[l].reshape(1, -1), ffn1_w_in[l].astype(BF16), ffn1_w_out[l].astype(BF16), fg,
            w_mix_in[l], ffn2_w_in[l], ffn2_w_out[l], w_mix_out[l])

        w_gates = jnp.pad(w_mix_in[l][:, W_HEAD:W_HEAD + W_GATES], ((0, 0), (0, GATE_PAD - W_GATES))).astype(BF16)
        gate_b = jnp.concatenate([mlstm_b_igate[l], mlstm_b_fgate[l]]).reshape(W_GATES, 1)
        h = _mixer(
            h, mix_norm_g[l].reshape(1, -1), w_head, w_tail, w_gates,
            mlstm_conv_w[l], mlstm_conv_b[l].reshape(1, -1), gate_b, mlstm_head_norm_g[l].reshape(1, -1),
            sgu_norm_g[l].reshape(1, -1), sgu_norm_b[l].reshape(1, -1), sgu_w_s[l], sgu_b_s[l].T,
            w_o, bsz=bsz, seq=seq)

        last = l == ffn1_w_in.shape[0] - 1
        h = _ffn(h, ffn2_norm_g[l].reshape(1, -1), w2_in, w2_out, fg, final_norm=last)
    return h.reshape(bsz, seq, D_MODEL)
```

```python
import functools

import jax
import jax.numpy as jnp
from jax import lax
from jax.experimental import pallas as pl
from jax.experimental.pallas import tpu as pltpu

D_MODEL = 1024
CHUNK = 64
M_HEADS = 4
M_HEAD_DIM = D_MODEL // M_HEADS
M_WIDTH = M_HEADS * M_HEAD_DIM
CONV_K = 4
SGU_BLOCK = 128
SGU_GROUPS = 4
SGU_WIDTH = D_MODEL
SGU_GROUP_DIM = SGU_WIDTH // SGU_GROUPS
D_FF = 2816
RMS_EPS = 1e-6
LN_EPS = 1e-5

MC = 512
TM = 512
SCAN_LAG = 3
GATE_PAD = 128
SUBLANES = 8
BF16_SUBLANES = 16
LANES = 128
FF_TILE = 256
VMEM_LIMIT = 56 * 1024 * 1024

F32 = jnp.float32
BF16 = jnp.bfloat16


def _dot(a, b):
    return jnp.dot(a, b, preferred_element_type=F32)


def _rmsnorm(x, g):
    return x * lax.rsqrt(jnp.mean(x * x, axis=-1, keepdims=True) + RMS_EPS) * g


def _layernorm(x, eps):
    xc = x - jnp.mean(x, axis=-1, keepdims=True)
    return xc * lax.rsqrt(jnp.mean(xc * xc, axis=-1, keepdims=True) + eps)


def _sigmoid(x):
    return 1.0 / (1.0 + jnp.exp(-x))


def _gelu(z):
    return 0.5 * z * (1.0 + lax.erf(z * (2.0 ** -0.5)))


def _ffn_body(x_ref, g_ref, win_ref, wout_ref, fg_ref, o_ref, h_ref, final_norm):
    x = x_ref[...]
    xb = _rmsnorm(x, g_ref[...]).astype(BF16)
    for j in range(D_FF // FF_TILE):
        lo = j * FF_TILE
        gate = _dot(xb, win_ref[:, lo:lo + FF_TILE].astype(BF16))
        up = _dot(xb, win_ref[:, D_FF + lo:D_FF + lo + FF_TILE].astype(BF16))
        h_ref[:, lo:lo + FF_TILE] = (gate * _sigmoid(gate) * up).astype(BF16)
    y = x + 0.5 * _dot(h_ref[...], wout_ref[...].astype(BF16))
    if final_norm:
        y = _rmsnorm(y, fg_ref[...])
    o_ref[...] = y


def _ffn_kernel(x_ref, g_ref, win_ref, wout_ref, fg_ref, o_ref, h_ref, *, final_norm):
    _ffn_body(x_ref, g_ref, win_ref, wout_ref, fg_ref, o_ref, h_ref, final_norm)


def _ffn_cast_kernel(x_ref, g_ref, win_ref, wout_ref, fg_ref, wt_head_ref, wt_tail_ref, wt_next_ref, wt_gate_ref,
                     wo_ref, o_ref, wa_ref, wb_ref, wif_ref, wo_bf_ref, h_ref):
    _ffn_body(x_ref, g_ref, win_ref, wout_ref, fg_ref, o_ref, h_ref, False)
    wa_ref[...] = wt_head_ref[...].T.astype(BF16)
    tail = jnp.concatenate([wt_tail_ref[W_GATES:, :], wt_next_ref[0:W_GATES, :]], axis=0)
    wb_ref[...] = tail.T.astype(BF16)
    gate_rows = jnp.concatenate([wt_gate_ref[...], jnp.zeros((GATE_PAD - W_GATES, D_MODEL), F32)], axis=0)
    wif_ref[...] = gate_rows.T.astype(BF16)
    wo_bf_ref[...] = wo_ref[...].astype(BF16)


def _ffn_specs(tm):
    const = lambda i: (0, 0)
    return [
        pl.BlockSpec((tm, D_MODEL), lambda i: (i, 0)),
        pl.BlockSpec((1, D_MODEL), const),
        pl.BlockSpec((D_MODEL, 2 * D_FF), const, pipeline_mode=pl.Buffered(1)),
        pl.BlockSpec((D_FF, D_MODEL), const, pipeline_mode=pl.Buffered(1)),
        pl.BlockSpec((1, D_MODEL), const),
    ]


def _ffn(x2d, g, w_in, w_out, final_g, *, final_norm, tm=512):
    t = x2d.shape[0]
    return pl.pallas_call(
        functools.partial(_ffn_kernel, final_norm=final_norm),
        grid=(t // tm,),
        in_specs=_ffn_specs(tm),
        out_specs=pl.BlockSpec((tm, D_MODEL), lambda i: (i, 0)),
        out_shape=jax.ShapeDtypeStruct((t, D_MODEL), F32),
        scratch_shapes=[pltpu.VMEM((tm, D_FF), BF16)],
        compiler_params=pltpu.CompilerParams(
            dimension_semantics=("parallel",), vmem_limit_bytes=VMEM_LIMIT),
        name="ffn_final" if final_norm else "ffn",
    )(x2d, g, w_in, w_out, final_g)


def _ffn_cast(x2d, g, w_in, w_out, final_g, wt_mix, layer, w_o, *, tm=512):
    t = x2d.shape[0]
    steps = t // tm
    assert W_HEAD == steps * LANES and W_TAIL == steps * LANES and D_MODEL % (steps * BF16_SUBLANES) == 0
    head_blocks = W_HEAD // LANES
    last_block = (W_HEAD + W_GATES + W_TAIL) // LANES
    col = lambda i: (0, i)
    bf = lambda shape: jax.ShapeDtypeStruct(shape, BF16)
    return pl.pallas_call(
        _ffn_cast_kernel,
        grid=(steps,),
        in_specs=_ffn_specs(tm) + [
            pl.BlockSpec((None, LANES, D_MODEL), lambda i: (layer, i, 0)),
            pl.BlockSpec((None, LANES, D_MODEL), lambda i: (layer, head_blocks + i, 0)),
            pl.BlockSpec((None, LANES, D_MODEL), lambda i: (layer, jnp.minimum(head_blocks + 1 + i, last_block), 0)),
            pl.BlockSpec((None, W_GATES, D_MODEL), lambda i: (layer, W_HEAD // W_GATES, 0)),
            pl.BlockSpec((D_MODEL // steps, D_MODEL), lambda i: (i, 0)),
        ],
        out_specs=[
            pl.BlockSpec((tm, D_MODEL), lambda i: (i, 0)),
            pl.BlockSpec((D_MODEL, LANES), col),
            pl.BlockSpec((D_MODEL, LANES), col),
            pl.BlockSpec((D_MODEL, GATE_PAD), lambda i: (0, 0)),
            pl.BlockSpec((D_MODEL // steps, D_MODEL), lambda i: (i, 0)),
        ],
        out_shape=[jax.ShapeDtypeStruct((t, D_MODEL), F32), bf((D_MODEL, W_HEAD)), bf((D_MODEL, W_TAIL)),
                   bf((D_MODEL, GATE_PAD)), bf((D_MODEL, D_MODEL))],
        scratch_shapes=[pltpu.VMEM((tm, D_FF), BF16)],
        compiler_params=pltpu.CompilerParams(
            dimension_semantics=("arbitrary",), vmem_limit_bytes=VMEM_LIMIT),
        name="ffn_cast",
    )(x2d, g, w_in, w_out, final_g, wt_mix, wt_mix, wt_mix, wt_mix, w_o)


W_HEAD = 4 * M_WIDTH
W_GATES = 2 * M_HEADS
W_TAIL = 2 * SGU_WIDTH + 2 * D_MODEL
_QK0, _V0, _O0 = 0, 2 * M_WIDTH, 3 * M_WIDTH
_Z0, _GM0, _GS0 = 0, 2 * SGU_WIDTH, 2 * SGU_WIDTH + D_MODEL


def _prefix_scan(x, op, lane):
    n = x.shape[-1]
    sh = 1
    while sh < n:
        x = jnp.where(lane >= sh, op(x, pltpu.roll(x, sh, axis=1)), x)
        sh *= 2
    return x


def _gate_vectors(gif, gb, m_ref):
    row = lax.broadcasted_iota(jnp.int32, (SUBLANES, MC), 0)
    lane = lax.broadcasted_iota(jnp.int32, (SUBLANES, MC), 1)
    head_rows = row < M_HEADS
    g8 = gif.T[0:SUBLANES, :] + gb
    lsig = jnp.minimum(g8, 0.0) - jnp.log1p(jnp.exp(-jnp.abs(g8)))
    lf = jnp.where(head_rows, pltpu.roll(lsig, M_HEADS, axis=0), 0.0)
    li = jnp.where(head_rows, g8, 0.0)
    b = _prefix_scan(lf, jnp.add, lane)
    r = li - b
    cm = _prefix_scan(r, jnp.maximum, lane)
    m_in = m_ref[:, 0:1]
    mx = jnp.maximum(m_in, cm)
    mx_last = mx[:, MC - 1:MC]
    m_ref[...] = jnp.broadcast_to(b[:, MC - 1:MC] + mx_last, m_ref.shape)
    cols = jnp.concatenate(
        [-mx, jnp.exp(m_in - mx), jnp.exp(-(b + mx)), jnp.exp(r - mx_last),
         jnp.zeros((LANES - 4 * SUBLANES, MC), F32)], axis=0).T
    return cols, r, jnp.exp(m_in - mx_last)


def _mixer_kernel(x_ref, g_ref, wa_ref, wb_ref, wif_ref, cw_ref, cb_ref, gb_ref, hg_ref, sg_ref, sb_ref, ws_ref,
                  bst_ref, wo_ref,
                  o_ref, buf_ref, ct_ref, n_ref, m_ref, qa_ref, ka_ref, va_ref, gmo_ref, sgu_ref, gu_ref, zv_ref,
                  vn_ref, xb_ref, *, tiles_per_seq):
    @pl.when(lax.rem(pl.program_id(0), tiles_per_seq) == 0)
    def _():
        buf_ref[0:SUBLANES, :] = jnp.zeros((SUBLANES, 2 * M_WIDTH), F32)
        ct_ref[...] = jnp.zeros(ct_ref.shape, F32)
        n_ref[...] = jnp.zeros(n_ref.shape, F32)
        m_ref[...] = jnp.zeros(m_ref.shape, F32)

    n_chunks = TM // MC
    xb_ref[...] = _rmsnorm(x_ref[...], g_ref[...]).astype(BF16)

    def head_cols(h):
        return slice(h * M_HEAD_DIM, (h + 1) * M_HEAD_DIM)

    def proj_head(lo):
        return _dot(xb_ref[...], wa_ref[:, lo:lo + M_HEAD_DIM])

    def proj_tail(lo):
        return _dot(xb_ref[...], wb_ref[:, lo:lo + M_HEAD_DIM])

    gif = _dot(xb_ref[...], wif_ref[...])
    gates = [_gate_vectors(gif[c * MC:(c + 1) * MC, :], gb_ref[...], m_ref) for c in range(n_chunks)]
    ti = lax.broadcasted_iota(jnp.int32, (MC, MC), 0)
    si = lax.broadcasted_iota(jnp.int32, (MC, MC), 1)
    causal = si <= ti

    def scan_unit(c, h):
        ts = slice(c * MC, (c + 1) * MC)
        sl = head_cols(h)
        cols, r_rows, decay_col = gates[c]
        q, k, v = qa_ref[ts, sl], ka_ref[ts, sl], va_ref[ts, sl]
        negmx = cols[:, h:h + 1]
        w_int = cols[:, SUBLANES + h:SUBLANES + h + 1]
        en = cols[:, 2 * SUBLANES + h:2 * SUBLANES + h + 1]
        wk = cols[:, 3 * SUBLANES + h:3 * SUBLANES + h + 1]
        r = r_rows[h:h + 1, :]
        decay = decay_col[h:h + 1, :]

        s = lax.dot_general(q, k, (((1,), (1,)), ((), ())), preferred_element_type=F32)
        ct = ct_ref[h]
        n = n_ref[h:h + 1, :]
        inter = _dot(q, ct.astype(BF16))
        kw = k.astype(F32) * wk
        ct_ref[h] = decay * ct + _dot(kw.T.astype(BF16), v)
        n_ref[h:h + 1, :] = decay * n + jnp.sum(kw, axis=0, keepdims=True)
        yield
        p = s * jnp.exp(jnp.where(causal, negmx + r, -jnp.inf))
        num = w_int * inter + _dot(p.astype(BF16), v)
        qn = jnp.sum(q.astype(F32) * n, axis=-1, keepdims=True)
        den = w_int * qn + jnp.sum(p, axis=-1, keepdims=True)
        hh = num / jnp.maximum(jnp.abs(den), en)
        yield
        merged = gmo_ref[ts, sl] * _layernorm(hh, LN_EPS) + sgu_ref[ts, sl]
        part = _dot(merged.astype(BF16), wo_ref[sl, :])
        o_ref[ts, :] = (x_ref[ts, :] if h == 0 else o_ref[ts, :]) + part
        yield

    def piece_u_gate(j):
        cs = head_cols(j)
        gu_ref[:, cs] = _sigmoid(proj_tail(_GS0 + cs.start)) * _gelu(proj_tail(_Z0 + cs.start))

    def piece_v_act(j):
        cs = head_cols(j)
        zv_ref[:, cs] = _gelu(proj_tail(_Z0 + SGU_WIDTH + cs.start))

    def piece_v_norm():
        vn_ref[...] = (_layernorm(zv_ref[...], LN_EPS) * sg_ref[...] + sb_ref[...]).astype(BF16)

    def piece_spatial(g):
        bi = lax.broadcasted_iota(jnp.int32, (SGU_BLOCK, SGU_BLOCK), 0) // CHUNK
        bj = lax.broadcasted_iota(jnp.int32, (SGU_BLOCK, SGU_BLOCK), 1) // CHUNK
        wg = jnp.where(bi >= bj, ws_ref[g], 0.0).astype(BF16)
        bg = bst_ref[:, g:g + 1]
        cs = head_cols(g)
        for nb in range(TM // SGU_BLOCK):
            rs = slice(nb * SGU_BLOCK, (nb + 1) * SGU_BLOCK)
            sgu_ref[rs, cs] = gu_ref[rs, cs] * (_dot(wg, vn_ref[rs, cs]) + bg)

    def piece_conv(h, base, dst, scale):
        cs = slice(base + h * M_HEAD_DIM, base + (h + 1) * M_HEAD_DIM)
        raw = proj_head(_QK0 + cs.start)
        buf_ref[SUBLANES:SUBLANES + TM, cs] = raw
        acc = raw * cw_ref[CONV_K - 1:CONV_K, cs] + cb_ref[:, cs]
        for j in range(CONV_K - 1):
            off = SUBLANES - (CONV_K - 1) + j
            acc = acc + buf_ref[off:off + TM, cs] * cw_ref[j:j + 1, cs]
        buf_ref[0:SUBLANES, cs] = buf_ref[TM:TM + SUBLANES, cs]
        dst[:, head_cols(h)] = (acc * _sigmoid(acc) * scale).astype(BF16)

    def piece_v(h):
        sl = head_cols(h)
        va_ref[:, sl] = proj_head(_V0 + sl.start).astype(BF16)

    def piece_gmo(h):
        sl = head_cols(h)
        gmo_ref[:, sl] = (_sigmoid(proj_tail(_GM0 + sl.start)) * _sigmoid(proj_head(_O0 + sl.start))
                          * hg_ref[:, sl])

    bind = functools.partial
    units = [(c, h) for h in range(M_HEADS) for c in range(n_chunks)]
    phases = 3
    order = [(t - p, p) for t in range(len(units) + phases - 1) for p in range(phases) if 0 <= t - p < len(units)]
    position = {up: b for b, up in enumerate(order)}
    queue = []
    for h in range(M_HEADS):
        first = max(position[(n_chunks * h, 0)] - SCAN_LAG, 0)
        final = max(position[(n_chunks * h, phases - 1)] - SCAN_LAG, 0)
        queue += [(first, bind(piece_conv, h, 0, qa_ref, 1.0)),
                  (first, bind(piece_conv, h, M_WIDTH, ka_ref, M_HEAD_DIM ** -0.5)),
                  (first, bind(piece_v, h))]
        if h == 0:
            queue += [(final, bind(piece_v_act, j)) for j in range(SGU_GROUPS)]
            queue += [(final, piece_v_norm)]
        queue += [(final, bind(piece_u_gate, h)), (final, bind(piece_spatial, h)), (final, bind(piece_gmo, h))]
    queue.sort(key=lambda item: item[0])

    total = len(queue)
    last = queue[-1][0]

    def emit(boundary):
        want = -(-total * (boundary + 1) // (last + 1))
        while queue and (queue[0][0] <= boundary or total - len(queue) < want):
            queue.pop(0)[1]()

    emit(0)
    running = [scan_unit(c, h) for c, h in units]
    for boundary, (u, _) in enumerate(order, start=1):
        next(running[u])
        emit(boundary)
    assert not queue


def _mixer(x1, g, w_head, w_tail, w_gates, conv_w, conv_b, gate_b, head_g, sgu_g, sgu_b, w_s, b_s_t, w_out,
           *, bsz, seq):
    t = bsz * seq
    row = lambda i: (i, 0)
    c2 = lambda i: (0, 0)
    c3 = lambda i: (0, 0, 0)
    return pl.pallas_call(
        functools.partial(_mixer_kernel, tiles_per_seq=seq // TM),
        grid=(t // TM,),
        in_specs=[
            pl.BlockSpec((TM, D_MODEL), row),
            pl.BlockSpec((1, D_MODEL), c2),
            pl.BlockSpec((D_MODEL, W_HEAD), c2, pipeline_mode=pl.Buffered(1)),
            pl.BlockSpec((D_MODEL, W_TAIL), c2, pipeline_mode=pl.Buffered(1)),
            pl.BlockSpec((D_MODEL, GATE_PAD), c2),
            pl.BlockSpec((CONV_K, 2 * M_WIDTH), c2),
            pl.BlockSpec((1, 2 * M_WIDTH), c2),
            pl.BlockSpec((SUBLANES, 1), c2),
            pl.BlockSpec((1, M_WIDTH), c2),
            pl.BlockSpec((1, SGU_WIDTH), c2),
            pl.BlockSpec((1, SGU_WIDTH), c2),
            pl.BlockSpec((SGU_GROUPS, SGU_BLOCK, SGU_BLOCK), c3),
            pl.BlockSpec((SGU_BLOCK, SGU_GROUPS), c2),
            pl.BlockSpec((D_MODEL, D_MODEL), c2, pipeline_mode=pl.Buffered(1)),
        ],
        out_specs=pl.BlockSpec((TM, D_MODEL), row),
        out_shape=jax.ShapeDtypeStruct((t, D_MODEL), F32),
        scratch_shapes=[
            pltpu.VMEM((TM + 2 * SUBLANES, 2 * M_WIDTH), F32),
            pltpu.VMEM((M_HEADS, M_HEAD_DIM, M_HEAD_DIM), F32),
            pltpu.VMEM((SUBLANES, M_HEAD_DIM), F32),
            pltpu.VMEM((SUBLANES, LANES), F32),
            pltpu.VMEM((TM, M_WIDTH), BF16),
            pltpu.VMEM((TM, M_WIDTH), BF16),
            pltpu.VMEM((TM, M_WIDTH), BF16),
            pltpu.VMEM((TM, M_WIDTH), F32),
            pltpu.VMEM((TM, SGU_WIDTH), F32),
            pltpu.VMEM((TM, SGU_WIDTH), F32),
            pltpu.VMEM((TM, SGU_WIDTH), F32),
            pltpu.VMEM((TM, SGU_WIDTH), BF16),
            pltpu.VMEM((TM, D_MODEL), BF16),
        ],
        compiler_params=pltpu.CompilerParams(
            dimension_semantics=("arbitrary",), vmem_limit_bytes=VMEM_LIMIT),
        name="mixer",
    )(x1, g, w_head, w_tail, w_gates, conv_w, conv_b, gate_b, head_g, sgu_g, sgu_b, w_s, b_s_t, w_out)


def kernel(x, ffn1_norm_g, ffn1_w_in, ffn1_w_out, mix_norm_g, w_mix_in, mlstm_conv_w, mlstm_conv_b, mlstm_b_igate, mlstm_b_fgate, mlstm_head_norm_g, sgu_norm_g, sgu_norm_b, sgu_w_s, sgu_b_s, w_mix_out, ffn2_norm_g, ffn2_w_in, ffn2_w_out, final_norm_g):
    bsz, seq, _ = x.shape
    h = x.reshape(bsz * seq, D_MODEL)
    fg = final_norm_g.reshape(1, D_MODEL)
    for l in range(ffn1_w_in.shape[0]):
        h, w_head, w_tail, w_gates, w_o = _ffn_cast(
            h, ffn1_norm_g[l].reshape(1, -1), ffn1_w_in[l], ffn1_w_out[l], fg,
            jnp.swapaxes(w_mix_in, 1, 2), l, w_mix_out[l])

        gate_b = jnp.concatenate([mlstm_b_igate[l], mlstm_b_fgate[l]]).reshape(W_GATES, 1)
        h = _mixer(
            h, mix_norm_g[l].reshape(1, -1), w_head, w_tail, w_gates,
            mlstm_conv_w[l], mlstm_conv_b[l].reshape(1, -1), gate_b, mlstm_head_norm_g[l].reshape(1, -1),
            sgu_norm_g[l].reshape(1, -1), sgu_norm_b[l].reshape(1, -1), sgu_w_s[l], sgu_b_s[l].T,
            w_o, bsz=bsz, seq=seq)

        last = l == ffn1_w_in.shape[0] - 1
        h = _ffn(h, ffn2_norm_g[l].reshape(1, -1), ffn2_w_in[l], ffn2_w_out[l], fg, final_norm=last)
    return h.reshape(bsz, seq, D_MODEL)
```

```python
import functools

import jax
import jax.numpy as jnp
from jax import lax
from jax.experimental import pallas as pl
from jax.experimental.pallas import tpu as pltpu

D_MODEL = 1024
CHUNK = 64
M_HEADS = 4
M_HEAD_DIM = D_MODEL // M_HEADS
M_WIDTH = M_HEADS * M_HEAD_DIM
CONV_K = 4
SGU_BLOCK = 128
SGU_GROUPS = 4
SGU_WIDTH = D_MODEL
SGU_GROUP_DIM = SGU_WIDTH // SGU_GROUPS
D_FF = 2816
RMS_EPS = 1e-6
LN_EPS = 1e-5

MC = 512
TM = 512
SCAN_LAG = 3
GATE_PAD = 128
SUBLANES = 8
BF16_SUBLANES = 16
LANES = 128
FF_TILE = 256
FFN2_TM = 1024
VMEM_LIMIT = 56 * 1024 * 1024

F32 = jnp.float32
BF16 = jnp.bfloat16


def _dot(a, b):
    return jnp.dot(a, b, preferred_element_type=F32)


def _rmsnorm(x, g):
    return x * lax.rsqrt(jnp.mean(x * x, axis=-1, keepdims=True) + RMS_EPS) * g


def _layernorm(x, eps):
    xc = x - jnp.mean(x, axis=-1, keepdims=True)
    return xc * lax.rsqrt(jnp.mean(xc * xc, axis=-1, keepdims=True) + eps)


def _sigmoid(x):
    return 1.0 / (1.0 + jnp.exp(-x))


def _as_bf16(w):
    return w if w.dtype == BF16 else w.astype(BF16)


def _gelu(z):
    return 0.5 * z * (1.0 + lax.erf(z * (2.0 ** -0.5)))


def _ffn_body(x_ref, g_ref, win_ref, wout_ref, fg_ref, o_ref, h_ref, final_norm):
    x = x_ref[...]
    xb = _rmsnorm(x, g_ref[...]).astype(BF16)
    for j in range(D_FF // FF_TILE):
        lo = j * FF_TILE
        gate = _dot(xb, _as_bf16(win_ref[:, lo:lo + FF_TILE]))
        up = _dot(xb, _as_bf16(win_ref[:, D_FF + lo:D_FF + lo + FF_TILE]))
        h_ref[:, lo:lo + FF_TILE] = (gate * _sigmoid(gate) * up).astype(BF16)
    y = x + 0.5 * _dot(h_ref[...], _as_bf16(wout_ref[...]))
    if final_norm:
        y = _rmsnorm(y, fg_ref[...])
    o_ref[...] = y


def _ffn_kernel(x_ref, g_ref, win_ref, wout_ref, fg_ref, o_ref, h_ref, *, final_norm):
    _ffn_body(x_ref, g_ref, win_ref, wout_ref, fg_ref, o_ref, h_ref, final_norm)


def _ffn_cast_kernel(x_ref, g_ref, win_ref, wout_ref, fg_ref, wt_head_ref, wt_tail_ref, wt_next_ref, wt_gate_ref,
                     wo_ref, w2in_ref, w2out_ref, o_ref, wa_ref, wb_ref, wif_ref, wo_bf_ref, w2in_bf_ref,
                     w2out_bf_ref, h_ref, *, w2out_steps):
    _ffn_body(x_ref, g_ref, win_ref, wout_ref, fg_ref, o_ref, h_ref, False)
    w2in_bf_ref[...] = w2in_ref[...].astype(BF16)

    @pl.when(pl.program_id(0) < w2out_steps)
    def _():
        w2out_bf_ref[...] = w2out_ref[...].astype(BF16)

    wa_ref[...] = wt_head_ref[...].T.astype(BF16)
    tail = jnp.concatenate([wt_tail_ref[W_GATES:, :], wt_next_ref[0:W_GATES, :]], axis=0)
    wb_ref[...] = tail.T.astype(BF16)
    gate_rows = jnp.concatenate([wt_gate_ref[...], jnp.zeros((GATE_PAD - W_GATES, D_MODEL), F32)], axis=0)
    wif_ref[...] = gate_rows.T.astype(BF16)
    wo_bf_ref[...] = wo_ref[...].astype(BF16)


def _ffn_specs(tm):
    const = lambda i: (0, 0)
    return [
        pl.BlockSpec((tm, D_MODEL), lambda i: (i, 0)),
        pl.BlockSpec((1, D_MODEL), const),
        pl.BlockSpec((D_MODEL, 2 * D_FF), const, pipeline_mode=pl.Buffered(1)),
        pl.BlockSpec((D_FF, D_MODEL), const, pipeline_mode=pl.Buffered(1)),
        pl.BlockSpec((1, D_MODEL), const),
    ]


def _ffn(x2d, g, w_in, w_out, final_g, *, final_norm, tm=512):
    t = x2d.shape[0]
    return pl.pallas_call(
        functools.partial(_ffn_kernel, final_norm=final_norm),
        grid=(t // tm,),
        in_specs=_ffn_specs(tm),
        out_specs=pl.BlockSpec((tm, D_MODEL), lambda i: (i, 0)),
        out_shape=jax.ShapeDtypeStruct((t, D_MODEL), F32),
        scratch_shapes=[pltpu.VMEM((tm, D_FF), BF16)],
        compiler_params=pltpu.CompilerParams(
            dimension_semantics=("parallel",), vmem_limit_bytes=VMEM_LIMIT),
        name="ffn_final" if final_norm else "ffn",
    )(x2d, g, w_in, w_out, final_g)


def _ffn_cast(x2d, g, w_in, w_out, final_g, wt_mix, layer, w_o, w2_in, w2_out, *, tm=512):
    t = x2d.shape[0]
    steps = t // tm
    assert W_HEAD == steps * LANES and W_TAIL == steps * LANES and D_MODEL % (steps * BF16_SUBLANES) == 0
    w2out_steps = steps // 2
    w2out_rows = D_FF // w2out_steps
    assert w2out_rows % BF16_SUBLANES == 0
    slab = lambda i: (i, 0)
    slab2 = lambda i: (jnp.minimum(i, w2out_steps - 1), 0)
    head_blocks = W_HEAD // LANES
    last_block = (W_HEAD + W_GATES + W_TAIL) // LANES
    col = lambda i: (0, i)
    bf = lambda shape: jax.ShapeDtypeStruct(shape, BF16)
    return pl.pallas_call(
        functools.partial(_ffn_cast_kernel, w2out_steps=w2out_steps),
        grid=(steps,),
        in_specs=_ffn_specs(tm) + [
            pl.BlockSpec((None, LANES, D_MODEL), lambda i: (layer, i, 0)),
            pl.BlockSpec((None, LANES, D_MODEL), lambda i: (layer, head_blocks + i, 0)),
            pl.BlockSpec((None, LANES, D_MODEL), lambda i: (layer, jnp.minimum(head_blocks + 1 + i, last_block), 0)),
            pl.BlockSpec((None, W_GATES, D_MODEL), lambda i: (layer, W_HEAD // W_GATES, 0)),
            pl.BlockSpec((D_MODEL // steps, D_MODEL), slab),
            pl.BlockSpec((D_MODEL // steps, 2 * D_FF), slab),
            pl.BlockSpec((w2out_rows, D_MODEL), slab2),
        ],
        out_specs=[
            pl.BlockSpec((tm, D_MODEL), lambda i: (i, 0)),
            pl.BlockSpec((D_MODEL, LANES), col),
            pl.BlockSpec((D_MODEL, LANES), col),
            pl.BlockSpec((D_MODEL, GATE_PAD), lambda i: (0, 0)),
            pl.BlockSpec((D_MODEL // steps, D_MODEL), slab),
            pl.BlockSpec((D_MODEL // steps, 2 * D_FF), slab),
            pl.BlockSpec((w2out_rows, D_MODEL), slab2),
        ],
        out_shape=[jax.ShapeDtypeStruct((t, D_MODEL), F32), bf((D_MODEL, W_HEAD)), bf((D_MODEL, W_TAIL)),
                   bf((D_MODEL, GATE_PAD)), bf((D_MODEL, D_MODEL)), bf((D_MODEL, 2 * D_FF)), bf((D_FF, D_MODEL))],
        scratch_shapes=[pltpu.VMEM((tm, D_FF), BF16)],
        compiler_params=pltpu.CompilerParams(
            dimension_semantics=("arbitrary",), vmem_limit_bytes=VMEM_LIMIT),
        name="ffn_cast",
    )(x2d, g, w_in, w_out, final_g, wt_mix, wt_mix, wt_mix, wt_mix, w_o, w2_in, w2_out)


W_HEAD = 4 * M_WIDTH
W_GATES = 2 * M_HEADS
W_TAIL = 2 * SGU_WIDTH + 2 * D_MODEL
_QK0, _V0, _O0 = 0, 2 * M_WIDTH, 3 * M_WIDTH
_Z0, _GM0, _GS0 = 0, 2 * SGU_WIDTH, 2 * SGU_WIDTH + D_MODEL


def _prefix_scan(x, op, lane):
    n = x.shape[-1]
    sh = 1
    while sh < n:
        x = jnp.where(lane >= sh, op(x, pltpu.roll(x, sh, axis=1)), x)
        sh *= 2
    return x


def _gate_vectors(gif, gb, m_ref):
    row = lax.broadcasted_iota(jnp.int32, (SUBLANES, MC), 0)
    lane = lax.broadcasted_iota(jnp.int32, (SUBLANES, MC), 1)
    head_rows = row < M_HEADS
    g8 = gif.T[0:SUBLANES, :] + gb
    lsig = jnp.minimum(g8, 0.0) - jnp.log1p(jnp.exp(-jnp.abs(g8)))
    lf = jnp.where(head_rows, pltpu.roll(lsig, M_HEADS, axis=0), 0.0)
    li = jnp.where(head_rows, g8, 0.0)
    b = _prefix_scan(lf, jnp.add, lane)
    r = li - b
    cm = _prefix_scan(r, jnp.maximum, lane)
    m_in = m_ref[:, 0:1]
    mx = jnp.maximum(m_in, cm)
    mx_last = mx[:, MC - 1:MC]
    m_ref[...] = jnp.broadcast_to(b[:, MC - 1:MC] + mx_last, m_ref.shape)
    cols = jnp.concatenate(
        [-mx, jnp.exp(m_in - mx), jnp.exp(-(b + mx)), jnp.exp(r - mx_last),
         jnp.zeros((LANES - 4 * SUBLANES, MC), F32)], axis=0).T
    return cols, r, jnp.exp(m_in - mx_last)


def _mixer_kernel(x_ref, g_ref, wa_ref, wb_ref, wif_ref, cw_ref, cb_ref, gb_ref, hg_ref, sg_ref, sb_ref, ws_ref,
                  bst_ref, wo_ref,
                  o_ref, buf_ref, ct_ref, n_ref, m_ref, qa_ref, ka_ref, va_ref, gmo_ref, sgu_ref, gu_ref, zv_ref,
                  vn_ref, xb_ref, *, tiles_per_seq):
    @pl.when(lax.rem(pl.program_id(0), tiles_per_seq) == 0)
    def _():
        buf_ref[0:SUBLANES, :] = jnp.zeros((SUBLANES, 2 * M_WIDTH), F32)
        ct_ref[...] = jnp.zeros(ct_ref.shape, F32)
        n_ref[...] = jnp.zeros(n_ref.shape, F32)
        m_ref[...] = jnp.zeros(m_ref.shape, F32)

    n_chunks = TM // MC
    xb_ref[...] = _rmsnorm(x_ref[...], g_ref[...]).astype(BF16)

    def head_cols(h):
        return slice(h * M_HEAD_DIM, (h + 1) * M_HEAD_DIM)

    def proj_head(lo):
        return _dot(xb_ref[...], wa_ref[:, lo:lo + M_HEAD_DIM])

    def proj_tail(lo):
        return _dot(xb_ref[...], wb_ref[:, lo:lo + M_HEAD_DIM])

    gif = _dot(xb_ref[...], wif_ref[...])
    gates = [_gate_vectors(gif[c * MC:(c + 1) * MC, :], gb_ref[...], m_ref) for c in range(n_chunks)]
    ti = lax.broadcasted_iota(jnp.int32, (MC, MC), 0)
    si = lax.broadcasted_iota(jnp.int32, (MC, MC), 1)
    causal = si <= ti

    def scan_unit(c, h):
        ts = slice(c * MC, (c + 1) * MC)
        sl = head_cols(h)
        cols, r_rows, decay_col = gates[c]
        q, k, v = qa_ref[ts, sl], ka_ref[ts, sl], va_ref[ts, sl]
        negmx = cols[:, h:h + 1]
        w_int = cols[:, SUBLANES + h:SUBLANES + h + 1]
        en = cols[:, 2 * SUBLANES + h:2 * SUBLANES + h + 1]
        wk = cols[:, 3 * SUBLANES + h:3 * SUBLANES + h + 1]
        r = r_rows[h:h + 1, :]
        decay = decay_col[h:h + 1, :]

        s = lax.dot_general(q, k, (((1,), (1,)), ((), ())), preferred_element_type=F32)
        ct = ct_ref[h]
        n = n_ref[h:h + 1, :]
        inter = _dot(q, ct.astype(BF16))
        kw = k.astype(F32) * wk
        ct_ref[h] = decay * ct + _dot(kw.T.astype(BF16), v)
        n_ref[h:h + 1, :] = decay * n + jnp.sum(kw, axis=0, keepdims=True)
        yield
        p = s * jnp.exp(jnp.where(causal, negmx + r, -jnp.inf))
        num = w_int * inter + _dot(p.astype(BF16), v)
        qn = jnp.sum(q.astype(F32) * n, axis=-1, keepdims=True)
        den = w_int * qn + jnp.sum(p, axis=-1, keepdims=True)
        hh = num / jnp.maximum(jnp.abs(den), en)
        yield
        merged = gmo_ref[ts, sl] * _layernorm(hh, LN_EPS) + sgu_ref[ts, sl]
        part = _dot(merged.astype(BF16), wo_ref[sl, :])
        o_ref[ts, :] = (x_ref[ts, :] if h == 0 else o_ref[ts, :]) + part
        yield

    def piece_u_gate(j):
        cs = head_cols(j)
        gu_ref[:, cs] = _sigmoid(proj_tail(_GS0 + cs.start)) * _gelu(proj_tail(_Z0 + cs.start))

    def piece_v_act(j):
        cs = head_cols(j)
        zv_ref[:, cs] = _gelu(proj_tail(_Z0 + SGU_WIDTH + cs.start))

    def piece_v_norm():
        vn_ref[...] = (_layernorm(zv_ref[...], LN_EPS) * sg_ref[...] + sb_ref[...]).astype(BF16)

    def piece_spatial(g):
        bi = lax.broadcasted_iota(jnp.int32, (SGU_BLOCK, SGU_BLOCK), 0) // CHUNK
        bj = lax.broadcasted_iota(jnp.int32, (SGU_BLOCK, SGU_BLOCK), 1) // CHUNK
        wg = jnp.where(bi >= bj, ws_ref[g], 0.0).astype(BF16)
        bg = bst_ref[:, g:g + 1]
        cs = head_cols(g)
        for nb in range(TM // SGU_BLOCK):
            rs = slice(nb * SGU_BLOCK, (nb + 1) * SGU_BLOCK)
            sgu_ref[rs, cs] = gu_ref[rs, cs] * (_dot(wg, vn_ref[rs, cs]) + bg)

    def piece_conv(h, base, dst, scale):
        cs = slice(base + h * M_HEAD_DIM, base + (h + 1) * M_HEAD_DIM)
        raw = proj_head(_QK0 + cs.start)
        buf_ref[SUBLANES:SUBLANES + TM, cs] = raw
        acc = raw * cw_ref[CONV_K - 1:CONV_K, cs] + cb_ref[:, cs]
        for j in range(CONV_K - 1):
            off = SUBLANES - (CONV_K - 1) + j
            acc = acc + buf_ref[off:off + TM, cs] * cw_ref[j:j + 1, cs]
        buf_ref[0:SUBLANES, cs] = buf_ref[TM:TM + SUBLANES, cs]
        dst[:, head_cols(h)] = (acc * _sigmoid(acc) * scale).astype(BF16)

    def piece_v(h):
        sl = head_cols(h)
        va_ref[:, sl] = proj_head(_V0 + sl.start).astype(BF16)

    def piece_gmo(h):
        sl = head_cols(h)
        gmo_ref[:, sl] = (_sigmoid(proj_tail(_GM0 + sl.start)) * _sigmoid(proj_head(_O0 + sl.start))
                          * hg_ref[:, sl])

    bind = functools.partial
    units = [(c, h) for h in range(M_HEADS) for c in range(n_chunks)]
    phases = 3
    order = [(t - p, p) for t in range(len(units) + phases - 1) for p in range(phases) if 0 <= t - p < len(units)]
    position = {up: b for b, up in enumerate(order)}
    queue = []
    for h in range(M_HEADS):
        first = max(position[(n_chunks * h, 0)] - SCAN_LAG, 0)
        final = max(position[(n_chunks * h, phases - 1)] - SCAN_LAG, 0)
        queue += [(first, bind(piece_conv, h, 0, qa_ref, 1.0)),
                  (first, bind(piece_conv, h, M_WIDTH, ka_ref, M_HEAD_DIM ** -0.5)),
                  (first, bind(piece_v, h))]
        if h == 0:
            queue += [(final, bind(piece_v_act, j)) for j in range(SGU_GROUPS)]
            queue += [(final, piece_v_norm)]
        queue += [(final, bind(piece_u_gate, h)), (final, bind(piece_spatial, h)), (final, bind(piece_gmo, h))]
    queue.sort(key=lambda item: item[0])

    total = len(queue)
    last = queue[-1][0]

    def emit(boundary):
        want = -(-total * (boundary + 1) // (last + 1))
        while queue and (queue[0][0] <= boundary or total - len(queue) < want):
            queue.pop(0)[1]()

    emit(0)
    running = [scan_unit(c, h) for c, h in units]
    for boundary, (u, _) in enumerate(order, start=1):
        next(running[u])
        emit(boundary)
    assert not queue


def _mixer(x1, g, w_head, w_tail, w_gates, conv_w, conv_b, gate_b, head_g, sgu_g, sgu_b, w_s, b_s_t, w_out,
           *, bsz, seq):
    t = bsz * seq
    row = lambda i: (i, 0)
    c2 = lambda i: (0, 0)
    c3 = lambda i: (0, 0, 0)
    return pl.pallas_call(
        functools.partial(_mixer_kernel, tiles_per_seq=seq // TM),
        grid=(t // TM,),
        in_specs=[
            pl.BlockSpec((TM, D_MODEL), row),
            pl.BlockSpec((1, D_MODEL), c2),
            pl.BlockSpec((D_MODEL, W_HEAD), c2, pipeline_mode=pl.Buffered(1)),
            pl.BlockSpec((D_MODEL, W_TAIL), c2, pipeline_mode=pl.Buffered(1)),
            pl.BlockSpec((D_MODEL, GATE_PAD), c2),
            pl.BlockSpec((CONV_K, 2 * M_WIDTH), c2),
            pl.BlockSpec((1, 2 * M_WIDTH), c2),
            pl.BlockSpec((SUBLANES, 1), c2),
            pl.BlockSpec((1, M_WIDTH), c2),
            pl.BlockSpec((1, SGU_WIDTH), c2),
            pl.BlockSpec((1, SGU_WIDTH), c2),
            pl.BlockSpec((SGU_GROUPS, SGU_BLOCK, SGU_BLOCK), c3),
            pl.BlockSpec((SGU_BLOCK, SGU_GROUPS), c2),
            pl.BlockSpec((D_MODEL, D_MODEL), c2, pipeline_mode=pl.Buffered(1)),
        ],
        out_specs=pl.BlockSpec((TM, D_MODEL), row),
        out_shape=jax.ShapeDtypeStruct((t, D_MODEL), F32),
        scratch_shapes=[
            pltpu.VMEM((TM + 2 * SUBLANES, 2 * M_WIDTH), F32),
            pltpu.VMEM((M_HEADS, M_HEAD_DIM, M_HEAD_DIM), F32),
            pltpu.VMEM((SUBLANES, M_HEAD_DIM), F32),
            pltpu.VMEM((SUBLANES, LANES), F32),
            pltpu.VMEM((TM, M_WIDTH), BF16),
            pltpu.VMEM((TM, M_WIDTH), BF16),
            pltpu.VMEM((TM, M_WIDTH), BF16),
            pltpu.VMEM((TM, M_WIDTH), F32),
            pltpu.VMEM((TM, SGU_WIDTH), F32),
            pltpu.VMEM((TM, SGU_WIDTH), F32),
            pltpu.VMEM((TM, SGU_WIDTH), F32),
            pltpu.VMEM((TM, SGU_WIDTH), BF16),
            pltpu.VMEM((TM, D_MODEL), BF16),
        ],
        compiler_params=pltpu.CompilerParams(
            dimension_semantics=("arbitrary",), vmem_limit_bytes=VMEM_LIMIT),
        name="mixer",
    )(x1, g, w_head, w_tail, w_gates, conv_w, conv_b, gate_b, head_g, sgu_g, sgu_b, w_s, b_s_t, w_out)


def kernel(x, ffn1_norm_g, ffn1_w_in, ffn1_w_out, mix_norm_g, w_mix_in, mlstm_conv_w, mlstm_conv_b, mlstm_b_igate, mlstm_b_fgate, mlstm_head_norm_g, sgu_norm_g, sgu_norm_b, sgu_w_s, sgu_b_s, w_mix_out, ffn2_norm_g, ffn2_w_in, ffn2_w_out, final_norm_g):
    bsz, seq, _ = x.shape
    h = x.reshape(bsz * seq, D_MODEL)
    fg = final_norm_g.reshape(1, D_MODEL)
    for l in range(ffn1_w_in.shape[0]):
        h, w_head, w_tail, w_gates, w_o, w2_in, w2_out = _ffn_cast(
            h, ffn1_norm_g[l].reshape(1, -1), ffn1_w_in[l], ffn1_w_out[l], fg,
            jnp.swapaxes(w_mix_in, 1, 2), l, w_mix_out[l], ffn2_w_in[l], ffn2_w_out[l])

        gate_b = jnp.concatenate([mlstm_b_igate[l], mlstm_b_fgate[l]]).reshape(W_GATES, 1)
        h = _mixer(
            h, mix_norm_g[l].reshape(1, -1), w_head, w_tail, w_gates,
            mlstm_conv_w[l], mlstm_conv_b[l].reshape(1, -1), gate_b, mlstm_head_norm_g[l].reshape(1, -1),
            sgu_norm_g[l].reshape(1, -1), sgu_norm_b[l].reshape(1, -1), sgu_w_s[l], sgu_b_s[l].T,
            w_o, bsz=bsz, seq=seq)

        last = l == ffn1_w_in.shape[0] - 1
        h = _ffn(h, ffn2_norm_g[l].reshape(1, -1), w2_in, w2_out, fg, final_norm=last, tm=FFN2_TM)
    return h.reshape(bsz, seq, D_MODEL)
```
